```python
import jax, jax.numpy as jnp
from jax import lax
import numpy as np

D_MODEL = 4096
BATCH = 32
SEQ = 256
DEPTH = 2
DEC_BATCH = 4
DEC_SEQ = 1024
PAST_LEN = 256

GRID_W = 64
HEAD_DIM = 128
NA_HEADS = 16
NA_DIM = NA_HEADS * HEAD_DIM
NA_WIN_ROWS = 8
NA_WIN_COLS = 16
NA_QBLK = 16
NA_KBLK = NA_QBLK + NA_WIN_COLS
LRU_WIDTH = 2048
LRU_BLOCKS = 16
LRU_BLOCK = LRU_WIDTH // LRU_BLOCKS
LRU_CONV = 4
LRU_C = 8.0
CONV_WIDTH = 2048
CONV_K = 31
MLA_HEADS = 16
Q_LORA = 1024
KV_LORA = 512
QK_NOPE = 128
QK_ROPE = 64
V_DIM = 128
ROPE_BASE = 10000.0
D_FF = 11008
N_SUB = 3
W_IN0 = 3 * NA_DIM + 2 * LRU_WIDTH
W_IN1 = 2 * CONV_WIDTH + Q_LORA + KV_LORA + QK_ROPE
EPS = 1e-6
ATTN_QBLK = 128
NEG_INF = -1e30

kernel_name = 'hybrid_flow_na_rglru_conformer_mla_step'


def rmsnorm(x, g):
    xf = x.astype(jnp.float32)
    y = xf * lax.rsqrt(jnp.mean(xf * xf, axis=-1, keepdims=True) + EPS)
    return (y * g.astype(jnp.float32)).astype(x.dtype)


def layernorm(x, g, b):
    xf = x.astype(jnp.float32)
    xc = xf - jnp.mean(xf, axis=-1, keepdims=True)
    var = jnp.mean(xc * xc, axis=-1, keepdims=True)
    return (xc * lax.rsqrt(var + EPS) * g.astype(jnp.float32) + b.astype(jnp.float32)).astype(x.dtype)


def modulation(cvec, ada_w, ada_b):
    m = jax.nn.silu(cvec) @ ada_w + ada_b
    return m.reshape(cvec.shape[0], 3 * N_SUB, D_MODEL)


def pre(x, mod, j, g):
    return rmsnorm(x, g) * (1.0 + mod[:, 3 * j + 1][:, None]) + mod[:, 3 * j][:, None]


def post(x, y, mod, j, g, weight):
    return x + weight * mod[:, 3 * j + 2][:, None] * rmsnorm(y, g)


def swiglu(h, w1, w2):
    g, u = jnp.split(h @ w1, 2, axis=-1)
    return (jax.nn.silu(g) * u) @ w2


def dwconv_centred(x, w, b):
    k = w.shape[0]
    y = lax.conv_general_dilated(x, w[:, None, :].astype(x.dtype), window_strides=(1,),
                                 padding=[((k - 1) // 2, k // 2)],
                                 dimension_numbers=('NWC', 'WIO', 'NWC'),
                                 feature_group_count=x.shape[-1])
    return y + b


def blocked_attention(q, k, v, scale):
    bsz, nh, lq, dq = q.shape
    nb = lq // ATTN_QBLK
    qb = q.reshape(bsz, nh, nb, ATTN_QBLK, dq).transpose(2, 0, 1, 3, 4)

    def one_block(qi):
        s = jnp.einsum('bhqd,bhkd->bhqk', qi, k, preferred_element_type=jnp.float32) * scale
        p = jax.nn.softmax(s, axis=-1).astype(v.dtype)
        return jnp.einsum('bhqk,bhkd->bhqd', p, v)

    o = lax.map(one_block, qb)
    return o.transpose(1, 2, 0, 3, 4).reshape(bsz, nh, lq, v.shape[-1])


def neighbourhood_attention(q, k, v, k_ctx, v_ctx, rpb):
    bsz, nh, n, hd = q.shape
    rows = n // GRID_W
    kr = min(NA_WIN_ROWS, rows)
    ncb = GRID_W // NA_QBLK
    scale = hd ** -0.5
    qg = q.reshape(bsz, nh, rows, ncb, NA_QBLK, hd)
    kg = k.reshape(bsz, nh, rows, GRID_W, hd)
    vg = v.reshape(bsz, nh, rows, GRID_W, hd)
    qcol = np.arange(GRID_W).reshape(ncb, NA_QBLK)
    cstart = np.clip(qcol - NA_WIN_COLS // 2, 0, GRID_W - NA_WIN_COLS)
    kcol = np.clip(cstart[:, :1], 0, GRID_W - NA_KBLK) + np.arange(NA_KBLK)
    col_ok = (kcol[:, None, :] >= cstart[:, :, None]) & (kcol[:, None, :] < cstart[:, :, None] + NA_WIN_COLS)
    col_idx = np.clip(kcol[:, None, :] - qcol[:, :, None] + NA_WIN_COLS - 1, 0, 2 * NA_WIN_COLS - 2)
    rpb_col = rpb[:, :, col_idx]
    n_loc = kr * NA_KBLK

    def one_row(r):
        rs = jnp.clip(r - kr // 2, 0, rows - kr)
        k_blk = lax.dynamic_slice_in_dim(kg, rs, kr, axis=2)[:, :, :, kcol]
        v_blk = lax.dynamic_slice_in_dim(vg, rs, kr, axis=2)[:, :, :, kcol]
        q_r = lax.dynamic_index_in_dim(qg, r, axis=2, keepdims=False)
        s_loc = jnp.einsum('bhjqd,bhkjcd->bhjqkc', q_r, k_blk, preferred_element_type=jnp.float32) * scale
        bias = jnp.take(rpb_col, rs + jnp.arange(kr) - r + NA_WIN_ROWS - 1, axis=1)
        s_loc = s_loc + bias.transpose(0, 2, 3, 1, 4).astype(jnp.float32)
        s_loc = jnp.where(col_ok[:, :, None, :], s_loc, NEG_INF)
        s_ctx = jnp.einsum('bhjqd,bhld->bhjql', q_r, k_ctx, preferred_element_type=jnp.float32) * scale
        s = jnp.concatenate([s_loc.reshape(bsz, nh, ncb, NA_QBLK, n_loc), s_ctx], axis=-1)
        p = jax.nn.softmax(s, axis=-1).astype(v.dtype)
        p_loc = p[..., :n_loc].reshape(bsz, nh, ncb, NA_QBLK, kr, NA_KBLK)
        return (jnp.einsum('bhjqkc,bhkjcd->bhjqd', p_loc, v_blk)
                + jnp.einsum('bhjql,bhld->bhjqd', p[..., n_loc:], v_ctx))

    o = lax.map(one_row, jnp.arange(rows))
    return o.transpose(1, 2, 0, 3, 4, 5).reshape(bsz, nh, n, hd)


def rglru(x, w_a, b_a, w_x, b_x, lam, h0, reverse):
    bsz, t, w = x.shape
    xb = x.reshape(bsz, t, LRU_BLOCKS, LRU_BLOCK)

    def gate(wm, bv):
        z = jnp.einsum('btni,nij->btnj', xb, wm.astype(jnp.float32)).reshape(bsz, t, w)
        return jax.nn.sigmoid(z + bv.astype(jnp.float32))

    r = gate(w_a, b_a)
    i = gate(w_x, b_x)
    log_a = -LRU_C * r * jax.nn.softplus(-lam.astype(jnp.float32))
    a = jnp.exp(log_a)
    u = jnp.sqrt(-jnp.expm1(2.0 * log_a)) * (i * x)

    def combine(lhs, rhs):
        return lhs[0] * rhs[0], rhs[0] * lhs[1] + rhs[1]

    a_cum, h = lax.associative_scan(combine, (a, u), reverse=reverse, axis=1)
    if h0 is not None:
        h = h + a_cum * h0.astype(jnp.float32)[:, None]
    final = h[:, 0] if reverse else h[:, -1]
    return h, final


def axial_rope_tables(n_tokens):
    t = jnp.arange(n_tokens)
    pos = jnp.stack([t // GRID_W, t % GRID_W], axis=-1).astype(jnp.float32)
    nf = QK_ROPE // 4
    inv = ROPE_BASE ** (-jnp.arange(nf, dtype=jnp.float32) / nf)
    ang = pos[:, :, None] * inv
    return jnp.cos(ang), jnp.sin(ang)


def apply_axial_rope(x, cos, sin):
    xr = x.reshape(x.shape[:-1] + (2, 2, QK_ROPE // 4)).astype(jnp.float32)
    x1, x2 = xr[..., 0, :], xr[..., 1, :]
    out = jnp.stack([x1 * cos - x2 * sin, x2 * cos + x1 * sin], axis=-2)
    return out.reshape(x.shape).astype(x.dtype)


def mixer_ab(h, ctx, w_in, rpb, conv_w, conv_b, wa, ba, wx, bx, lam, w_out):
    bsz, t, _ = h.shape
    cut = [NA_DIM, 2 * NA_DIM, 3 * NA_DIM, 3 * NA_DIM + LRU_WIDTH]
    q, k, v, x_rec, x_gate = jnp.split(h @ w_in, cut, axis=-1)
    heads = lambda z: z.reshape(bsz, t, NA_HEADS, HEAD_DIM).transpose(0, 2, 1, 3)
    q, k, v = heads(q), heads(k), heads(v)
    xc = dwconv_centred(x_rec, conv_w, conv_b).astype(jnp.float32)
    if ctx is None:
        attn = blocked_attention(q, k, v, HEAD_DIM ** -0.5)
        h_f, s_f = rglru(xc, wa[0], ba[0], wx[0], bx[0], lam[0], None, False)
        h_b, s_b = rglru(xc, wa[1], ba[1], wx[1], bx[1], lam[1], None, True)
        state = (k, v, jnp.stack([s_f, s_b], axis=1).astype(h.dtype))
    else:
        k_ctx, v_ctx, s0 = ctx
        attn = neighbourhood_attention(q, k, v, k_ctx.astype(k.dtype), v_ctx.astype(v.dtype), rpb)
        h_f, _ = rglru(xc, wa[0], ba[0], wx[0], bx[0], lam[0], s0[:, 0], False)
        h_b, _ = rglru(xc, wa[1], ba[1], wx[1], bx[1], lam[1], s0[:, 1], True)
        state = None
    rec = (h_f + h_b).astype(h.dtype) * jax.nn.gelu(x_gate)
    attn = attn.transpose(0, 2, 1, 3).reshape(bsz, t, NA_DIM)
    y = jnp.concatenate([attn, rec], axis=-1) @ w_out
    return y, state


def mixer_cd(h, ctx, w_in, dw_w, dw_b, ln_g, ln_b, qn_g, w_uq, kvn_g, w_ukv, w_out):
    bsz, t, _ = h.shape
    cut = [CONV_WIDTH, 2 * CONV_WIDTH, 2 * CONV_WIDTH + Q_LORA, 2 * CONV_WIDTH + Q_LORA + KV_LORA]
    g_val, g_gate, c_q, c_kv, k_pe = jnp.split(h @ w_in, cut, axis=-1)
    u = g_val * jax.nn.sigmoid(g_gate)
    u = jax.nn.silu(layernorm(dwconv_centred(u, dw_w, dw_b), ln_g, ln_b))
    q = (rmsnorm(c_q, qn_g) @ w_uq).reshape(bsz, t, MLA_HEADS, QK_NOPE + QK_ROPE).transpose(0, 2, 1, 3)
    q_nope, q_pe = q[..., :QK_NOPE], q[..., QK_NOPE:]
    c_kv = rmsnorm(c_kv, kvn_g)
    if ctx is None:
        kv_lat, kpe_all = c_kv, k_pe
        state = (c_kv, k_pe)
    else:
        ckv_ctx, kpe_ctx = ctx
        cos, sin = axial_rope_tables(t)
        q_pe = apply_axial_rope(q_pe, cos, sin)
        kv_lat = jnp.concatenate([c_kv, ckv_ctx.astype(c_kv.dtype)], axis=1)
        kpe_all = jnp.concatenate([apply_axial_rope(k_pe, cos, sin), kpe_ctx.astype(k_pe.dtype)], axis=1)
        state = None
    nk = kv_lat.shape[1]
    kv = (kv_lat @ w_ukv).reshape(bsz, nk, MLA_HEADS, QK_NOPE + V_DIM).transpose(0, 2, 1, 3)
    k_full = jnp.concatenate([kv[..., :QK_NOPE],
                              jnp.broadcast_to(kpe_all[:, None], (bsz, MLA_HEADS, nk, QK_ROPE))], axis=-1)
    q_full = jnp.concatenate([q_nope, q_pe], axis=-1)
    o = blocked_attention(q_full, k_full, kv[..., QK_NOPE:], (QK_NOPE + QK_ROPE) ** -0.5)
    o = o.transpose(0, 2, 1, 3).reshape(bsz, t, MLA_HEADS * V_DIM)
    y = jnp.concatenate([u, o], axis=-1) @ w_out
    return y, state


def run_layer(x, mod, norm_g, ffn_w1, ffn_w2, mixer):
    x = post(x, swiglu(pre(x, mod, 0, norm_g[0]), ffn_w1[0], ffn_w2[0]), mod, 0, norm_g[1], 0.5)
    y, state = mixer(pre(x, mod, 1, norm_g[2]))
    x = post(x, y, mod, 1, norm_g[3], 1.0)
    x = post(x, swiglu(pre(x, mod, 2, norm_g[4]), ffn_w1[1], ffn_w2[1]), mod, 2, norm_g[5], 0.5)
    return x, state


def setup_inputs(seed: int = 0) -> dict:
    key = jax.random.key(seed)
    ks = iter(jax.random.split(key, 48))
    D = D_MODEL

    def nrm(shape, scale=1.0):
        return jax.random.normal(next(ks), shape, jnp.float32) * scale

    def gain(shape):
        return 1.0 + nrm(shape, 0.05)

    def lam_init():
        u = jax.random.uniform(next(ks), (2, LRU_WIDTH), jnp.float32, 0.9, 0.999)
        a = u ** (1.0 / LRU_C)
        return jnp.log(a) - jnp.log1p(-a)

    return {
        'x_prompt': nrm((BATCH, SEQ, D)),
        'x_sample': nrm((DEC_BATCH, DEC_SEQ, D)),
        'c': nrm((DEC_BATCH, D)),
        'cache_l0_na_k': nrm((DEC_BATCH, NA_HEADS, PAST_LEN, HEAD_DIM)),
        'cache_l0_na_v': nrm((DEC_BATCH, NA_HEADS, PAST_LEN, HEAD_DIM)),
        'state_l0_lru': nrm((DEC_BATCH, 2, LRU_WIDTH), 0.5),
        'cache_l1_mla_ckv': nrm((DEC_BATCH, PAST_LEN, KV_LORA)),
        'cache_l1_mla_kpe': nrm((DEC_BATCH, PAST_LEN, QK_ROPE)),
        'c_ctx': nrm((D,)),
        'ada_w0': nrm((D, 3 * N_SUB * D), 0.5 * D ** -0.5),
        'ada_b0': nrm((3 * N_SUB * D,), 0.02),
        'norm_g0': gain((2 * N_SUB, D)),
        'ffn_w1_0': nrm((2, D, 2 * D_FF), D ** -0.5),
        'ffn_w2_0': nrm((2, D_FF, D), D_FF ** -0.5),
        'w_in0': nrm((D, W_IN0), D ** -0.5),
        'na_rpb0': nrm((NA_HEADS, 2 * NA_WIN_ROWS - 1, 2 * NA_WIN_COLS - 1), 0.1),
        'lru_conv_w0': nrm((LRU_CONV, LRU_WIDTH), LRU_CONV ** -0.5),
        'lru_conv_b0': nrm((LRU_WIDTH,), 0.02),
        'lru_wa0': nrm((2, LRU_BLOCKS, LRU_BLOCK, LRU_BLOCK), LRU_BLOCK ** -0.5),
        'lru_ba0': nrm((2, LRU_WIDTH), 0.02),
        'lru_wx0': nrm((2, LRU_BLOCKS, LRU_BLOCK, LRU_BLOCK), LRU_BLOCK ** -0.5),
        'lru_bx0': nrm((2, LRU_WIDTH), 0.02),
        'lru_lam0': lam_init(),
        'w_out0': nrm((NA_DIM + LRU_WIDTH, D), (NA_DIM + LRU_WIDTH) ** -0.5),
        'ada_w1': nrm((D, 3 * N_SUB * D), 0.5 * D ** -0.5),
        'ada_b1': nrm((3 * N_SUB * D,), 0.02),
        'norm_g1': gain((2 * N_SUB, D)),
        'ffn_w1_1': nrm((2, D, 2 * D_FF), D ** -0.5),
        'ffn_w2_1': nrm((2, D_FF, D), D_FF ** -0.5),
        'w_in1': nrm((D, W_IN1), D ** -0.5),
        'cv_dw_w1': nrm((CONV_K, CONV_WIDTH), CONV_K ** -0.5),
        'cv_dw_b1': nrm((CONV_WIDTH,), 0.02),
        'cv_ln_g1': gain((CONV_WIDTH,)),
        'cv_ln_b1': nrm((CONV_WIDTH,), 0.02),
        'mla_qnorm_g1': gain((Q_LORA,)),
        'mla_w_uq1': nrm((Q_LORA, MLA_HEADS * (QK_NOPE + QK_ROPE)), Q_LORA ** -0.5),
        'mla_kvnorm_g1': gain((KV_LORA,)),
        'mla_w_ukv1': nrm((KV_LORA, MLA_HEADS * (QK_NOPE + V_DIM)), KV_LORA ** -0.5),
        'w_out1': nrm((CONV_WIDTH + MLA_HEADS * V_DIM, D), (CONV_WIDTH + MLA_HEADS * V_DIM) ** -0.5),
    }


def reference(x_prompt, x_sample, c, cache_l0_na_k, cache_l0_na_v, state_l0_lru, cache_l1_mla_ckv,
              cache_l1_mla_kpe, c_ctx,
              ada_w0, ada_b0, norm_g0, ffn_w1_0, ffn_w2_0, w_in0, na_rpb0, lru_conv_w0, lru_conv_b0,
              lru_wa0, lru_ba0, lru_wx0, lru_bx0, lru_lam0, w_out0,
              ada_w1, ada_b1, norm_g1, ffn_w1_1, ffn_w2_1, w_in1, cv_dw_w1, cv_dw_b1, cv_ln_g1, cv_ln_b1,
              mla_qnorm_g1, mla_w_uq1, mla_kvnorm_g1, mla_w_ukv1, w_out1):
    ada = [(ada_w0, ada_b0), (ada_w1, ada_b1)]
    norms = [norm_g0, norm_g1]
    ffns = [(ffn_w1_0, ffn_w2_0), (ffn_w1_1, ffn_w2_1)]
    mixers = [mixer_ab, mixer_cd]
    mixer_w = [
        (w_in0, na_rpb0, lru_conv_w0, lru_conv_b0, lru_wa0, lru_ba0, lru_wx0, lru_bx0, lru_lam0, w_out0),
        (w_in1, cv_dw_w1, cv_dw_b1, cv_ln_g1, cv_ln_b1, mla_qnorm_g1, mla_w_uq1, mla_kvnorm_g1,
         mla_w_ukv1, w_out1),
    ]
    ctx_caches = [(cache_l0_na_k, cache_l0_na_v, state_l0_lru), (cache_l1_mla_ckv, cache_l1_mla_kpe)]

    y_p, y_s = x_prompt, x_sample
    ctx_out = []
    for l in range(DEPTH):
        mix = mixers[l % 2]
        wts = mixer_w[l]
        mod_p = modulation(c_ctx[None], *ada[l])
        mod_s = modulation(c, *ada[l])
        y_p, st = run_layer(y_p, mod_p, norms[l], *ffns[l], lambda hh: mix(hh, None, *wts))
        y_s, _ = run_layer(y_s, mod_s, norms[l], *ffns[l], lambda hh: mix(hh, ctx_caches[l], *wts))
        ctx_out.append(st)

    (new_l0_na_k, new_l0_na_v, new_l0_lru), (new_l1_mla_ckv, new_l1_mla_kpe) = ctx_out
    return (y_p, y_s, new_l0_na_k, new_l0_na_v, new_l0_lru, new_l1_mla_ckv, new_l1_mla_kpe)
```

```python
import functools

import numpy as np
import jax
import jax.numpy as jnp
from jax import lax
from jax.experimental import pallas as pl
from jax.experimental.pallas import tpu as pltpu

F32 = jnp.float32
BF16 = jnp.bfloat16

EPS = 1e-6
NEG_INF = -1e30
GRID_W = 64
LRU_C = 8.0
ROPE_BASE = 10000.0
N_SUB = 3
LANES = 128
SUBLANES = 8
MIB = 2 ** 20


def _cparams(sem, vmem_mib):
    return pltpu.CompilerParams(dimension_semantics=sem, vmem_limit_bytes=int(vmem_mib * MIB))


def _pick(n, pref, mult=LANES):
    best = None
    d = mult
    while d <= min(n, pref):
        if n % d == 0:
            best = d
        d += mult
    return best if best is not None else n


def _silu(x):
    return x * jax.nn.sigmoid(x)


def _rms(x, g):
    return x * lax.rsqrt(jnp.mean(x * x, axis=-1, keepdims=True) + EPS) * g


def _mod_kernel(c_ref, w_ref, b_ref, o_ref):
    s = _silu(c_ref[...]).astype(BF16)
    o_ref[...] = jnp.dot(s, w_ref[...].astype(BF16), preferred_element_type=F32) + b_ref[...]


def _modulation(cvec, ada_w, ada_b):
    r, d = cvec.shape
    n = ada_w.shape[1]
    bn = _pick(n, 512)
    return pl.pallas_call(
        _mod_kernel,
        grid=(n // bn,),
        in_specs=[pl.BlockSpec((r, d), lambda j: (0, 0)),
                  pl.BlockSpec((d, bn), lambda j: (0, j)),
                  pl.BlockSpec((1, bn), lambda j: (0, j))],
        out_specs=pl.BlockSpec((r, bn), lambda j: (0, j)),
        out_shape=jax.ShapeDtypeStruct((r, n), F32),
        compiler_params=_cparams(("arbitrary",), 2 * d * bn * 4 / MIB + 8),
        name="modulation",
    )(cvec, ada_w, ada_b.reshape(1, n))


def _pre_kernel(x_ref, g_ref, sh_ref, sc_ref, h_ref):
    y = _rms(x_ref[...], g_ref[0])
    h_ref[...] = (y * (1.0 + sc_ref[0]) + sh_ref[0]).astype(BF16)


def _postpre_kernel(nk, weight, has_next, *refs):
    if has_next:
        x_ref, y_ref, gate_ref, gpost_ref, sh_ref, sc_ref, gpre_ref, xo_ref, h_ref = refs
    else:
        x_ref, y_ref, gate_ref, gpost_ref, xo_ref = refs
    y = y_ref[0]
    for k in range(1, nk):
        y = y + y_ref[k]
    xn = x_ref[...] + (weight * gate_ref[0]) * _rms(y, gpost_ref[0])
    xo_ref[...] = xn
    if has_next:
        h_ref[...] = (_rms(xn, gpre_ref[0]) * (1.0 + sc_ref[0]) + sh_ref[0]).astype(BF16)


class _Stream:
    def __init__(self, mp, seq, ms, dec_seq):
        self.mp, self.seq, self.ms, self.dec_seq = mp, seq, ms, dec_seq
        self.m = mp + ms

    def mod_row(self, i, tb):
        pos = i * tb
        return jnp.where(pos < self.mp, 0, 1 + (pos - self.mp) // self.dec_seq)


def _mod_spec(st, tb, d, comp):
    return pl.BlockSpec((1, 1, d), lambda i: (st.mod_row(i, tb) * (3 * N_SUB) + comp, 0, 0))


def _gain_spec(d, k):
    return pl.BlockSpec((1, 1, d), lambda i: (k, 0, 0))


def _pre(st, x, mod3, norm3, j, tb=256):
    m, d = x.shape
    return pl.pallas_call(
        _pre_kernel,
        grid=(m // tb,),
        in_specs=[pl.BlockSpec((tb, d), lambda i: (i, 0)),
                  _gain_spec(d, 2 * j),
                  _mod_spec(st, tb, d, 3 * j),
                  _mod_spec(st, tb, d, 3 * j + 1)],
        out_specs=pl.BlockSpec((tb, d), lambda i: (i, 0)),
        out_shape=jax.ShapeDtypeStruct((m, d), BF16),
        compiler_params=_cparams(("arbitrary",), 6 * tb * d * 4 / MIB + 8),
        name="pre",
    )(x, norm3, mod3, mod3)


def _postpre(st, x, y, mod3, norm3, j, weight, nxt=None, tb=128):
    m, d = x.shape
    nk = y.shape[0]
    in_specs = [pl.BlockSpec((tb, d), lambda i: (i, 0)),
                pl.BlockSpec((nk, tb, d), lambda i: (0, i, 0)),
                _mod_spec(st, tb, d, 3 * j + 2),
                _gain_spec(d, 2 * j + 1)]
    args = [x, y, mod3, norm3]
    out_specs = [pl.BlockSpec((tb, d), lambda i: (i, 0))]
    out_shape = [jax.ShapeDtypeStruct((m, d), F32)]
    if nxt is not None:
        nmod3, nnorm3, nj = nxt
        in_specs += [_mod_spec(st, tb, d, 3 * nj), _mod_spec(st, tb, d, 3 * nj + 1), _gain_spec(d, 2 * nj)]
        args += [nmod3, nmod3, nnorm3]
        out_specs.append(pl.BlockSpec((tb, d), lambda i: (i, 0)))
        out_shape.append(jax.ShapeDtypeStruct((m, d), BF16))
    out = pl.pallas_call(
        functools.partial(_postpre_kernel, nk, weight, nxt is not None),
        grid=(m // tb,),
        in_specs=in_specs,
        out_specs=out_specs,
        out_shape=out_shape,
        compiler_params=_cparams(("arbitrary",), (2 * (nk + 4) + 4) * tb * d * 4 / MIB + 8),
        name="postpre",
    )(*args)
    return out if nxt is not None else (out[0], None)


def _mm_kernel(out_dtype, x_ref, w_ref, o_ref, wbf_ref):
    @pl.when(pl.program_id(2) == 0)
    def _():
        wbf_ref[...] = w_ref[...].astype(BF16)

    o_ref[0] = jnp.dot(x_ref[...], wbf_ref[...], preferred_element_type=F32).astype(out_dtype)


def _ws_matmul(x, w3, widx, n_out, *, n_off=0, bm=1024, bn=512, nk=1, out_dtype=F32, name="matmul"):
    m, k = x.shape
    bk = k // nk
    bm = _pick(m, bm, SUBLANES)
    bn = _pick(n_out, bn)
    assert k % nk == 0 and bk % LANES == 0 and n_off % bn == 0
    joff = n_off // bn
    osz = jnp.dtype(out_dtype).itemsize
    vmem = (2 * bk * bn * 4 + bk * bn * 2 + 2 * bm * bk * 2 + 2 * bm * bn * osz + bm * bn * 4) / MIB + 6
    return pl.pallas_call(
        functools.partial(_mm_kernel, out_dtype),
        grid=(nk, n_out // bn, m // bm),
        in_specs=[pl.BlockSpec((bm, bk), lambda kk, j, i: (i, kk)),
                  pl.BlockSpec((None, bk, bn), lambda kk, j, i: (widx, kk, j + joff))],
        out_specs=pl.BlockSpec((1, bm, bn), lambda kk, j, i: (kk, i, j)),
        out_shape=jax.ShapeDtypeStruct((nk, m, n_out), out_dtype),
        scratch_shapes=[pltpu.VMEM((bk, bn), BF16)],
        compiler_params=_cparams(("arbitrary", "arbitrary", "arbitrary"), vmem),
        name=name,
    )(x, w3)


def _mm_swiglu_kernel(x_ref, wg_ref, wu_ref, o_ref, wg_bf, wu_bf):
    @pl.when(pl.program_id(1) == 0)
    def _():
        wg_bf[...] = wg_ref[...].astype(BF16)
        wu_bf[...] = wu_ref[...].astype(BF16)

    x = x_ref[...]
    g = jnp.dot(x, wg_bf[...], preferred_element_type=F32)
    u = jnp.dot(x, wu_bf[...], preferred_element_type=F32)
    o_ref[...] = (_silu(g) * u).astype(BF16)


def _ws_matmul_swiglu(x, w3, widx, *, bm=1024, bn=256):
    m, k = x.shape
    f = w3.shape[2] // 2
    bm = _pick(m, bm, SUBLANES)
    bn = _pick(f, bn)
    nj = f // bn
    vmem = (4 * k * bn * 4 + 2 * k * bn * 2 + 2 * bm * k * 2 + 2 * bm * bn * 2 + 3 * bm * bn * 4) / MIB + 6
    return pl.pallas_call(
        _mm_swiglu_kernel,
        grid=(nj, m // bm),
        in_specs=[pl.BlockSpec((bm, k), lambda j, i: (i, 0)),
                  pl.BlockSpec((None, k, bn), lambda j, i: (widx, 0, j)),
                  pl.BlockSpec((None, k, bn), lambda j, i: (widx, 0, j + nj))],
        out_specs=pl.BlockSpec((bm, bn), lambda j, i: (i, j)),
        out_shape=jax.ShapeDtypeStruct((m, f), BF16),
        scratch_shapes=[pltpu.VMEM((k, bn), BF16), pltpu.VMEM((k, bn), BF16)],
        compiler_params=_cparams(("arbitrary", "arbitrary"), vmem),
        name="matmul_swiglu",
    )(x, w3, w3)


def _dot_nt(a, b):
    return lax.dot_general(a, b, (((1,), (1,)), ((), ())), preferred_element_type=F32)


def _softmax_pv(scores, values):
    mx = functools.reduce(jnp.maximum, [jnp.max(s, axis=-1, keepdims=True) for s in scores])
    ps = [jnp.exp(s - mx) for s in scores]
    den = functools.reduce(lambda a, b: a + b, [jnp.sum(p, axis=-1, keepdims=True) for p in ps])
    acc = None
    for p, v in zip(ps, values):
        o = jnp.dot(p.astype(BF16), v, preferred_element_type=F32)
        acc = o if acc is None else acc + o
    return acc / den


def _na_ctx_kernel(nh, hd, q_ref, k_ref, v_ref, o_ref, ko_ref, vo_ref):
    scale = hd ** -0.5
    for h in range(nh):
        sl = slice(h * hd, (h + 1) * hd)
        k = k_ref[:, sl]
        v = v_ref[:, sl]
        ko_ref[0, h] = k
        vo_ref[0, h] = v
        s = _dot_nt(q_ref[:, sl].astype(BF16), k.astype(BF16)) * scale
        o_ref[:, sl] = _softmax_pv([s], [v.astype(BF16)]).astype(BF16)


def _na_ctx_attention(z, nb, seq, nh, hd):
    na = nh * hd
    return pl.pallas_call(
        functools.partial(_na_ctx_kernel, nh, hd),
        grid=(nb,),
        in_specs=[pl.BlockSpec((seq, na), lambda b: (b, 0)),
                  pl.BlockSpec((seq, na), lambda b: (b, 1)),
                  pl.BlockSpec((seq, na), lambda b: (b, 2))],
        out_specs=[pl.BlockSpec((seq, na), lambda b: (b, 0)),
                   pl.BlockSpec((1, nh, seq, hd), lambda b: (b, 0, 0, 0)),
                   pl.BlockSpec((1, nh, seq, hd), lambda b: (b, 0, 0, 0))],
        out_shape=[jax.ShapeDtypeStruct((nb * seq, na), BF16),
                   jax.ShapeDtypeStruct((nb, nh, seq, hd), F32),
                   jax.ShapeDtypeStruct((nb, nh, seq, hd), F32)],
        compiler_params=_cparams(("arbitrary",), 12 * seq * na * 4 / MIB + 8),
        name="na_ctx_attention",
    )(z, z, z)


def _na_window(rows, kr, r):
    return min(max(r - kr // 2, 0), rows - kr)


def _na_lat_kernel(rows, kr, gw, hd, q_ref, k_ref, v_ref, kc_ref, vc_ref, bias_ref, ok_ref, o_ref):
    scale = hd ** -0.5
    kc = kc_ref[0, 0].astype(BF16)
    vc = vc_ref[0, 0].astype(BF16)
    ok = ok_ref[...] > 0.5
    for r in range(rows):
        rs = _na_window(rows, kr, r)
        q = q_ref[r * gw:(r + 1) * gw, :].astype(BF16)
        k = k_ref[rs * gw:(rs + kr) * gw, :].astype(BF16)
        v = v_ref[rs * gw:(rs + kr) * gw, :].astype(BF16)
        s_loc = jnp.where(ok, _dot_nt(q, k) * scale + bias_ref[0, r], NEG_INF)
        s_ctx = _dot_nt(q, kc) * scale
        o_ref[r * gw:(r + 1) * gw, :] = _softmax_pv([s_loc, s_ctx], [v, vc]).astype(BF16)


def _na_bias_tables(rpb, rows, gw):
    nh, nr, nc = rpb.shape
    wr, wc = (nr + 1) // 2, (nc + 1) // 2
    kr = min(wr, rows)
    qc = np.arange(gw)[:, None]
    kc = np.arange(gw)[None, :]
    cstart = np.clip(qc - wc // 2, 0, gw - wc)
    ok = (kc >= cstart) & (kc < cstart + wc)
    cidx = np.clip(kc - qc + wc - 1, 0, 2 * wc - 2)
    rpb_col = rpb[:, :, cidx]
    tabs = []
    for r in range(rows):
        rs = _na_window(rows, kr, r)
        ridx = rs + np.arange(kr) - r + wr - 1
        t = rpb_col[:, ridx]
        tabs.append(t.transpose(0, 2, 1, 3).reshape(nh, gw, kr * gw))
    ok_tab = np.tile(ok.astype(np.float32), (1, kr))
    return jnp.stack(tabs, axis=1), jnp.asarray(ok_tab), kr


def _na_lat_attention(z, row_blk0, nb, t, nh, hd, k_ctx, v_ctx, rpb):
    gw = GRID_W
    rows = t // gw
    bias, ok_tab, kr = _na_bias_tables(rpb, rows, gw)
    past = k_ctx.shape[2]
    return pl.pallas_call(
        functools.partial(_na_lat_kernel, rows, kr, gw, hd),
        grid=(nb, nh),
        in_specs=[pl.BlockSpec((t, hd), lambda b, h: (row_blk0 + b, h)),
                  pl.BlockSpec((t, hd), lambda b, h: (row_blk0 + b, nh + h)),
                  pl.BlockSpec((t, hd), lambda b, h: (row_blk0 + b, 2 * nh + h)),
                  pl.BlockSpec((1, 1, past, hd), lambda b, h: (b, h, 0, 0)),
                  pl.BlockSpec((1, 1, past, hd), lambda b, h: (b, h, 0, 0)),
                  pl.BlockSpec((1, rows, gw, kr * gw), lambda b, h: (h, 0, 0, 0)),
                  pl.BlockSpec((gw, kr * gw), lambda b, h: (0, 0))],
        out_specs=pl.BlockSpec((t, hd), lambda b, h: (b, h)),
        out_shape=jax.ShapeDtypeStruct((nb * t, nh * hd), BF16),
        compiler_params=_cparams(("arbitrary", "arbitrary"), 32),
        name="na_latent_attention",
    )(z, z, z, k_ctx, v_ctx, bias, ok_tab)


def _gelu_tanh(x):
    return 0.5 * x * (1.0 + jnp.tanh(0.7978845608028654 * (x + 0.044715 * (x * x * x))))


def _lru_kernel(t, wc, ntap, has_h0, rc, *refs):
    if has_h0:
        (xr_ref, xg_ref, cw_ref, cb_ref, wa_ref, ba_ref, wx_ref, bx_ref, lam_ref, h0_ref,
         rec_ref, xp_s, xc_s, a_s, u_s, h_s) = refs
    else:
        (xr_ref, xg_ref, cw_ref, cb_ref, wa_ref, ba_ref, wx_ref, bx_ref, lam_ref,
         rec_ref, st_ref, xp_s, xc_s, a_s, u_s, h_s) = refs
    pad = SUBLANES
    left = (ntap - 1) // 2
    nblk = wc // LANES
    nchunks = t // rc
    ntiles = t // SUBLANES

    xp_s[0:pad, :] = jnp.zeros((pad, wc), F32)
    xp_s[pad + t:pad + t + pad, :] = jnp.zeros((pad, wc), F32)
    xp_s[pad:pad + t, :] = xr_ref[...]

    for r0 in range(0, t, rc):
        acc = jnp.broadcast_to(cb_ref[...], (rc, wc))
        for i in range(ntap):
            base = r0 + pad - left + i
            acc = acc + cw_ref[i:i + 1, :] * xp_s[base:base + rc, :]
        xc_s[r0:r0 + rc, :] = acc

    row = lax.broadcasted_iota(jnp.int32, (SUBLANES, wc), 0)

    for d in range(2):
        reverse = d == 1
        sp = jax.nn.softplus(-lam_ref[d:d + 1, :])

        def gate_chunk(c, carry):
            r0 = pl.multiple_of(c * rc, rc)
            xc = xc_s[pl.ds(r0, rc), :]
            xcb = xc.astype(BF16)
            zs_a, zs_x = [], []
            for n in range(nblk):
                xb = xcb[:, n * LANES:(n + 1) * LANES]
                zs_a.append(jnp.dot(xb, wa_ref[d, n].astype(BF16), preferred_element_type=F32))
                zs_x.append(jnp.dot(xb, wx_ref[d, n].astype(BF16), preferred_element_type=F32))
            rg = jax.nn.sigmoid(jnp.concatenate(zs_a, axis=1) + ba_ref[d:d + 1, :])
            ig = jax.nn.sigmoid(jnp.concatenate(zs_x, axis=1) + bx_ref[d:d + 1, :])
            log_a = (-LRU_C) * rg * sp
            a_s[pl.ds(r0, rc), :] = jnp.exp(log_a)
            u_s[pl.ds(r0, rc), :] = jnp.sqrt(1.0 - jnp.exp(2.0 * log_a)) * (ig * xc)
            return carry

        lax.fori_loop(0, nchunks, gate_chunk, 0)

        def scan_tile(i, h_prev):
            tile = (ntiles - 1 - i) if reverse else i
            r0 = pl.multiple_of(tile * SUBLANES, SUBLANES)
            a = a_s[pl.ds(r0, SUBLANES), :]
            u = u_s[pl.ds(r0, SUBLANES), :]
            for s in (1, 2, 4):
                if reverse:
                    keep = row < SUBLANES - s
                    shift = SUBLANES - s
                else:
                    keep = row >= s
                    shift = s
                a_sh = jnp.where(keep, pltpu.roll(a, shift, 0), 1.0)
                u_sh = jnp.where(keep, pltpu.roll(u, shift, 0), 0.0)
                u = u + a * u_sh
                a = a * a_sh
            h = u + a * h_prev
            if reverse:
                h_s[pl.ds(r0, SUBLANES), :] = h_s[pl.ds(r0, SUBLANES), :] + h
                return h[0:1, :]
            h_s[pl.ds(r0, SUBLANES), :] = h
            return h[SUBLANES - 1:SUBLANES, :]

        h_init = h0_ref[0, d:d + 1, :] if has_h0 else jnp.zeros((1, wc), F32)
        h_last = lax.fori_loop(0, ntiles, scan_tile, h_init)
        if not has_h0:
            st_ref[0, d:d + 1, :] = h_last

    def out_chunk(c, carry):
        r0 = pl.multiple_of(c * rc, rc)
        rec_ref[pl.ds(r0, rc), :] = (h_s[pl.ds(r0, rc), :] * _gelu_tanh(xg_ref[pl.ds(r0, rc), :])).astype(BF16)
        return carry

    lax.fori_loop(0, nchunks, out_chunk, 0)


def _lru(z, col0, row_blk0, nb, t, width, cw, cb, wa, ba, wx, bx, lam, h0):
    wc = _pick(width, 512)
    ncb = width // wc
    ntap = cw.shape[0]
    nblk = wc // LANES
    has_h0 = h0 is not None
    rc = 64
    in_specs = [pl.BlockSpec((t, wc), lambda b, c: (row_blk0 + b, col0 // wc + c)),
                pl.BlockSpec((t, wc), lambda b, c: (row_blk0 + b, (col0 + width) // wc + c)),
                pl.BlockSpec((ntap, wc), lambda b, c: (0, c)),
                pl.BlockSpec((1, wc), lambda b, c: (0, c)),
                pl.BlockSpec((2, nblk, LANES, LANES), lambda b, c: (0, c, 0, 0)),
                pl.BlockSpec((2, wc), lambda b, c: (0, c)),
                pl.BlockSpec((2, nblk, LANES, LANES), lambda b, c: (0, c, 0, 0)),
                pl.BlockSpec((2, wc), lambda b, c: (0, c)),
                pl.BlockSpec((2, wc), lambda b, c: (0, c))]
    args = [z, z, cw, cb.reshape(1, width), wa, ba, wx, bx, lam]
    out_specs = [pl.BlockSpec((t, wc), lambda b, c: (b, c))]
    out_shape = [jax.ShapeDtypeStruct((nb * t, width), BF16)]
    if has_h0:
        in_specs.append(pl.BlockSpec((1, 2, wc), lambda b, c: (b, 0, c)))
        args.append(h0)
    else:
        out_specs.append(pl.BlockSpec((1, 2, wc), lambda b, c: (b, 0, c)))
        out_shape.append(jax.ShapeDtypeStruct((nb, 2, width), F32))
    out = pl.pallas_call(
        functools.partial(_lru_kernel, t, wc, ntap, has_h0, rc),
        grid=(nb, ncb),
        in_specs=in_specs,
        out_specs=out_specs,
        out_shape=out_shape,
        scratch_shapes=[pltpu.VMEM((t + 2 * SUBLANES, wc), F32)] + [pltpu.VMEM((t, wc), F32)] * 4,
        compiler_params=_cparams(("arbitrary", "arbitrary"), 12 * t * wc * 4 / MIB + 10),
        name="rglru",
    )(*args)
    return out if not has_h0 else (out[0], None)


_HALO = 16


def _conv_module_kernel(st, tb, ntap, cch, rch, v_ref, g_ref, vp_ref, gp_ref, vn_ref, gn_ref,
                        w_ref, b_ref, lg_ref, lb_ref, o_ref, u_s, y_s):
    i = pl.program_id(0)
    pos = i * tb
    in_ctx = pos < st.mp
    off = jnp.where(in_ctx, pos % st.seq, (pos - st.mp) % st.dec_seq)
    length = jnp.where(in_ctx, st.seq, st.dec_seq)
    has_prev = (off != 0).astype(F32)
    has_next = (off + tb != length).astype(F32)
    c = v_ref.shape[1]
    left = (ntap - 1) // 2

    u_s[0:_HALO, :] = vp_ref[...] * jax.nn.sigmoid(gp_ref[...]) * has_prev
    u_s[_HALO:_HALO + tb, :] = v_ref[...] * jax.nn.sigmoid(g_ref[...])
    u_s[_HALO + tb:_HALO + tb + _HALO, :] = vn_ref[...] * jax.nn.sigmoid(gn_ref[...]) * has_next

    for r0 in range(0, tb, rch):
        for c0 in range(0, c, cch):
            acc = jnp.broadcast_to(b_ref[:, c0:c0 + cch], (rch, cch))
            for k in range(ntap):
                base = _HALO - left + k + r0
                acc = acc + w_ref[k:k + 1, c0:c0 + cch] * u_s[base:base + rch, c0:c0 + cch]
            y_s[r0:r0 + rch, c0:c0 + cch] = acc

    y = y_s[...]
    yc = y - jnp.mean(y, axis=-1, keepdims=True)
    var = jnp.mean(yc * yc, axis=-1, keepdims=True)
    o_ref[...] = _silu(yc * lax.rsqrt(var + EPS) * lg_ref[...] + lb_ref[...]).astype(BF16)


def _conv_module(st, z, c, w, b, ln_g, ln_b, tb=64):
    m = z.shape[0]
    ntap = w.shape[0]
    assert (ntap - 1) // 2 < _HALO and ntap // 2 < _HALO
    hpb = tb // _HALO
    nhb = m // _HALO
    prev_map = lambda col: (lambda i: (jnp.maximum(i * hpb - 1, 0), col))
    next_map = lambda col: (lambda i: (jnp.minimum((i + 1) * hpb, nhb - 1), col))
    vec = lambda: pl.BlockSpec((1, c), lambda i: (0, 0))
    return pl.pallas_call(
        functools.partial(_conv_module_kernel, st, tb, ntap, _pick(c, 512), 32),
        grid=(m // tb,),
        in_specs=[pl.BlockSpec((tb, c), lambda i: (i, 0)),
                  pl.BlockSpec((tb, c), lambda i: (i, 1)),
                  pl.BlockSpec((_HALO, c), prev_map(0)),
                  pl.BlockSpec((_HALO, c), prev_map(1)),
                  pl.BlockSpec((_HALO, c), next_map(0)),
                  pl.BlockSpec((_HALO, c), next_map(1)),
                  pl.BlockSpec((ntap, c), lambda i: (0, 0)),
                  vec(), vec(), vec()],
        out_specs=pl.BlockSpec((tb, c), lambda i: (i, 0)),
        out_shape=jax.ShapeDtypeStruct((m, c), BF16),
        scratch_shapes=[pltpu.VMEM((tb + 2 * _HALO, c), F32), pltpu.VMEM((tb, c), F32)],
        compiler_params=_cparams(("arbitrary",), 24),
        name="conv_module",
    )(z, z, z, z, z, z, w, b.reshape(1, c), ln_g.reshape(1, c), ln_b.reshape(1, c))


def _mla_norm_kernel(cq_ref, ckv_ref, gq_ref, gkv_ref, q_ref, kv_ref, kvb_ref):
    q_ref[...] = _rms(cq_ref[...], gq_ref[...]).astype(BF16)
    kv = _rms(ckv_ref[...], gkv_ref[...])
    kv_ref[...] = kv
    kvb_ref[...] = kv.astype(BF16)


def _mla_norm(z, col_q, ql, col_kv, kvl, gq, gkv, tb=256):
    m = z.shape[0]
    return pl.pallas_call(
        _mla_norm_kernel,
        grid=(m // tb,),
        in_specs=[pl.BlockSpec((tb, ql), lambda i: (i, col_q // ql)),
                  pl.BlockSpec((tb, kvl), lambda i: (i, col_kv // kvl)),
                  pl.BlockSpec((1, ql), lambda i: (0, 0)),
                  pl.BlockSpec((1, kvl), lambda i: (0, 0))],
        out_specs=[pl.BlockSpec((tb, ql), lambda i: (i, 0)),
                   pl.BlockSpec((tb, kvl), lambda i: (i, 0)),
                   pl.BlockSpec((tb, kvl), lambda i: (i, 0))],
        out_shape=[jax.ShapeDtypeStruct((m, ql), BF16),
                   jax.ShapeDtypeStruct((m, kvl), F32),
                   jax.ShapeDtypeStruct((m, kvl), BF16)],
        compiler_params=_cparams(("arbitrary",), 16),
        name="mla_norm",
    )(z, z, gq.reshape(1, ql), gkv.reshape(1, kvl))


def _swap_halves(x, quarter):
    n = x.shape[-1]
    lane = lax.broadcasted_iota(jnp.int32, x.shape, x.ndim - 1)
    first = (lane % (2 * quarter)) < quarter
    return jnp.where(first, pltpu.roll(x, n - quarter, x.ndim - 1), pltpu.roll(x, quarter, x.ndim - 1))


def _rope_kernel(quarter, rope, qpe_ref, kpe_ref, cq_ref, sq_ref, ck_ref, sk_ref, qo_ref, ko_ref):
    q = qpe_ref[...]
    qo_ref[...] = (q * cq_ref[...] + _swap_halves(q, quarter) * sq_ref[...]).astype(BF16)
    k = kpe_ref[...]
    kr = k * ck_ref[...] + _swap_halves(k, quarter) * sk_ref[...]
    ko_ref[...] = (kr + pltpu.roll(kr, rope, 1)).astype(BF16)


def _rope_tables(t, rope, nrep):
    tok = jnp.arange(t)
    pos = jnp.stack([tok // GRID_W, tok % GRID_W], axis=-1).astype(F32)
    nf = rope // 4
    inv = ROPE_BASE ** (-jnp.arange(nf, dtype=F32) / nf)
    ang = pos[:, :, None] * inv
    cos, sin = jnp.cos(ang), jnp.sin(ang)
    cos_t = jnp.concatenate([cos[:, 0], cos[:, 0], cos[:, 1], cos[:, 1]], axis=-1)
    sin_t = jnp.concatenate([-sin[:, 0], sin[:, 0], -sin[:, 1], sin[:, 1]], axis=-1)
    return jnp.tile(cos_t, (1, nrep)), jnp.tile(sin_t, (1, nrep))


def _rope(q, col_pe, pe_w, kpe, row0, ms, t, rope, tb=256):
    assert 2 * rope == LANES and kpe.shape[1] == LANES
    cq, sq = _rope_tables(t, rope, pe_w // rope)
    ck, sk = _rope_tables(t, rope, 1)
    zpad = jnp.zeros((t, LANES - rope), F32)
    ck = jnp.concatenate([ck, zpad], axis=1)
    sk = jnp.concatenate([sk, zpad], axis=1)
    rb0 = row0 // tb
    tpb = t // tb
    return pl.pallas_call(
        functools.partial(_rope_kernel, rope // 4, rope),
        grid=(ms // tb,),
        in_specs=[pl.BlockSpec((tb, pe_w), lambda i: (rb0 + i, col_pe // pe_w)),
                  pl.BlockSpec((tb, LANES), lambda i: (rb0 + i, 0)),
                  pl.BlockSpec((tb, pe_w), lambda i: (i % tpb, 0)),
                  pl.BlockSpec((tb, pe_w), lambda i: (i % tpb, 0)),
                  pl.BlockSpec((tb, LANES), lambda i: (i % tpb, 0)),
                  pl.BlockSpec((tb, LANES), lambda i: (i % tpb, 0))],
        out_specs=[pl.BlockSpec((tb, pe_w), lambda i: (i, 0)),
                   pl.BlockSpec((tb, LANES), lambda i: (i, 0))],
        out_shape=[jax.ShapeDtypeStruct((ms, pe_w), BF16),
                   jax.ShapeDtypeStruct((ms, LANES), BF16)],
        compiler_params=_cparams(("arbitrary",), 24),
        name="rope",
    )(q, kpe, cq, sq, ck, sk)


def _mla_ctx_kernel(nh, dn, dv, rope, qn_ref, qpe_ref, kv_ref, kpe_ref, o_ref):
    scale = (dn + rope) ** -0.5
    kpe = kpe_ref[...]
    kpe2 = (kpe + pltpu.roll(kpe, rope, 1)).astype(BF16)
    lane = lax.broadcasted_iota(jnp.int32, (qpe_ref.shape[0], LANES), 1)
    for h in range(nh):
        qn = qn_ref[:, h * dn:(h + 1) * dn].astype(BF16)
        kn = kv_ref[:, h * (dn + dv):h * (dn + dv) + dn]
        v = kv_ref[:, h * (dn + dv) + dn:(h + 1) * (dn + dv)]
        blk = (h * rope) // LANES
        half = (h * rope) % LANES // rope
        qp = qpe_ref[:, blk * LANES:(blk + 1) * LANES]
        qp = jnp.where(lane // rope == half, qp, 0.0).astype(BF16)
        s = (_dot_nt(qn, kn) + _dot_nt(qp, kpe2)) * scale
        o_ref[:, h * dv:(h + 1) * dv] = _softmax_pv([s], [v]).astype(BF16)


def _mla_ctx_attention(q, kv, kpe, nb, seq, nh, dn, dv, rope):
    assert dn == dv
    return pl.pallas_call(
        functools.partial(_mla_ctx_kernel, nh, dn, dv, rope),
        grid=(nb,),
        in_specs=[pl.BlockSpec((seq, nh * dn), lambda b: (b, 0)),
                  pl.BlockSpec((seq, nh * rope), lambda b: (b, dn // rope)),
                  pl.BlockSpec((seq, nh * (dn + dv)), lambda b: (b, 0)),
                  pl.BlockSpec((seq, LANES), lambda b: (b, 0))],
        out_specs=pl.BlockSpec((seq, nh * dv), lambda b: (b, 0)),
        out_shape=jax.ShapeDtypeStruct((nb * seq, nh * dv), BF16),
        compiler_params=_cparams(("arbitrary",), 32),
        name="mla_ctx_attention",
    )(q, q, kv, kpe)


def _mla_lat_kernel(t, qc, dn, dv, rope, qn_ref, qpe_ref, kv_ref, kvc_ref, kpe_ref, kpec_ref, o_ref):
    scale = (dn + rope) ** -0.5
    hps = LANES // rope
    kpe = kpe_ref[...]
    kpec = kpec_ref[0]
    lane = lax.broadcasted_iota(jnp.int32, (qc, LANES), 1)
    for hh in range(hps):
        c0 = hh * (dn + dv)
        kn, v = kv_ref[:, c0:c0 + dn], kv_ref[:, c0 + dn:c0 + dn + dv]
        knc, vc = kvc_ref[:, c0:c0 + dn], kvc_ref[:, c0 + dn:c0 + dn + dv]
        for r0 in range(0, t, qc):
            qn = qn_ref[r0:r0 + qc, hh * dn:(hh + 1) * dn].astype(BF16)
            qp = jnp.where(lane // rope == hh, qpe_ref[r0:r0 + qc, :], jnp.zeros((), BF16))
            s_own = (_dot_nt(qn, kn) + _dot_nt(qp, kpe)) * scale
            s_ctx = (_dot_nt(qn, knc) + _dot_nt(qp, kpec)) * scale
            o_ref[r0:r0 + qc, hh * dv:(hh + 1) * dv] = _softmax_pv([s_own, s_ctx], [v, vc]).astype(BF16)


def _mla_lat_attention(q, row_blk0, qpe, kv, kv_ctx, kpe, kpe_ctx, nb, t, nh, dn, dv, rope):
    assert dn == dv == LANES
    hps = LANES // rope
    past = kv_ctx.shape[0] // nb
    return pl.pallas_call(
        functools.partial(_mla_lat_kernel, t, _pick(t, 256, SUBLANES), dn, dv, rope),
        grid=(nb, nh // hps),
        in_specs=[pl.BlockSpec((t, hps * dn), lambda b, j: (row_blk0 + b, j)),
                  pl.BlockSpec((t, LANES), lambda b, j: (b, j)),
                  pl.BlockSpec((t, hps * (dn + dv)), lambda b, j: (row_blk0 + b, j)),
                  pl.BlockSpec((past, hps * (dn + dv)), lambda b, j: (b, j)),
                  pl.BlockSpec((t, LANES), lambda b, j: (b, 0)),
                  pl.BlockSpec((1, past, LANES), lambda b, j: (b, 0, 0))],
        out_specs=pl.BlockSpec((t, hps * dv), lambda b, j: (b, j)),
        out_shape=jax.ShapeDtypeStruct((nb * t, nh * dv), BF16),
        compiler_params=_cparams(("arbitrary", "arbitrary"), 32),
        name="mla_latent_attention",
    )(q, qpe, kv, kv_ctx, kpe, kpe_ctx)


def _mixer_ab(st, h, nb_ctx, nb_lat, k_ctx, v_ctx, s0, w_in, rpb, conv_w, conv_b, wa, ba, wx, bx, lam, w_out):
    nh, hd = k_ctx.shape[1], k_ctx.shape[3]
    na = nh * hd
    width = lam.shape[1]
    z = _ws_matmul(h, w_in[None], 0, w_in.shape[1], name="w_in0")[0]
    attn_c, new_k, new_v = _na_ctx_attention(z, nb_ctx, st.seq, nh, hd)
    attn_l = _na_lat_attention(z, st.mp // st.dec_seq, nb_lat, st.dec_seq, nh, hd, k_ctx, v_ctx, rpb)
    rec_c, new_s = _lru(z, 3 * na, 0, nb_ctx, st.seq, width, conv_w, conv_b, wa, ba, wx, bx, lam, None)
    rec_l, _ = _lru(z, 3 * na, st.mp // st.dec_seq, nb_lat, st.dec_seq, width,
                    conv_w, conv_b, wa, ba, wx, bx, lam, s0)
    mix = jnp.concatenate([jnp.concatenate([attn_c, attn_l], axis=0),
                           jnp.concatenate([rec_c, rec_l], axis=0)], axis=1)
    y = _ws_matmul(mix, w_out[None], 0, w_out.shape[1], name="w_out0")
    return y, (new_k, new_v, new_s)


def _mixer_cd(st, h, nb_ctx, nb_lat, ckv_ctx, kpe_ctx, w_in, dw_w, dw_b, ln_g, ln_b, qn_g, w_uq, kvn_g,
              w_ukv, w_out):
    cw = dw_w.shape[1]
    ql = qn_g.shape[0]
    kvl = kvn_g.shape[0]
    rope = kpe_ctx.shape[2]
    dn = dv = LANES
    nh = w_ukv.shape[1] // (dn + dv)
    n_main = 2 * cw + ql + kvl
    z = _ws_matmul(h, w_in[None], 0, n_main, name="w_in1")[0]
    w_kpe = jnp.pad(w_in[:, n_main:], ((0, 0), (0, LANES - rope)))
    kpe = _ws_matmul(h, w_kpe[None], 0, LANES, bn=LANES, name="w_kpe")[0]

    u = _conv_module(st, z, cw, dw_w, dw_b, ln_g, ln_b)

    cq_n, ckv_n, ckv_nb = _mla_norm(z, 2 * cw, ql, 2 * cw + ql, kvl, qn_g, kvn_g)
    w_uq3 = w_uq.reshape(ql, nh, dn + rope)
    w_uq_perm = jnp.concatenate([w_uq3[:, :, :dn].reshape(ql, nh * dn),
                                 w_uq3[:, :, dn:].reshape(ql, nh * rope)], axis=1)
    q = _ws_matmul(cq_n, w_uq_perm[None], 0, w_uq_perm.shape[1], bn=1024, name="w_uq")[0]
    kv = _ws_matmul(ckv_nb, w_ukv[None], 0, w_ukv.shape[1], bn=1024, out_dtype=BF16, name="w_ukv")[0]
    past = ckv_ctx.shape[1]
    kv_ctx = _ws_matmul(ckv_ctx.reshape(nb_lat * past, kvl).astype(BF16), w_ukv[None], 0, w_ukv.shape[1],
                        bn=1024, out_dtype=BF16, name="w_ukv_ctx")[0]
    qpe_l, kpe_l = _rope(q, nh * dn, nh * rope, kpe, st.mp, st.ms, st.dec_seq, rope)
    kpe_c2 = jnp.concatenate([kpe_ctx, kpe_ctx], axis=-1).astype(BF16)
    o_c = _mla_ctx_attention(q, kv, kpe, nb_ctx, st.seq, nh, dn, dv, rope)
    o_l = _mla_lat_attention(q, st.mp // st.dec_seq, qpe_l, kv, kv_ctx, kpe_l, kpe_c2,
                             nb_lat, st.dec_seq, nh, dn, dv, rope)
    mix = jnp.concatenate([u, jnp.concatenate([o_c, o_l], axis=0)], axis=1)
    y = _ws_matmul(mix, w_out[None], 0, w_out.shape[1], name="w_out1")
    new_ckv = ckv_n[:st.mp].reshape(nb_ctx, st.seq, kvl)
    new_kpe = kpe[:st.mp, :rope].reshape(nb_ctx, st.seq, rope)
    return y, (new_ckv, new_kpe)


def _ffn(h, w1, w2, f):
    a = _ws_matmul_swiglu(h, w1, f)
    return _ws_matmul(a, w2, f, w2.shape[2], bm=512, bn=512, nk=2, name="ffn_w2")


def kernel(x_prompt, x_sample, c, cache_l0_na_k, cache_l0_na_v, state_l0_lru, cache_l1_mla_ckv, cache_l1_mla_kpe, c_ctx, ada_w0, ada_b0, norm_g0, ffn_w1_0, ffn_w2_0, w_in0, na_rpb0, lru_conv_w0, lru_conv_b0, lru_wa0, lru_ba0, lru_wx0, lru_bx0, lru_lam0, w_out0, ada_w1, ada_b1, norm_g1, ffn_w1_1, ffn_w2_1, w_in1, cv_dw_w1, cv_dw_b1, cv_ln_g1, cv_ln_b1, mla_qnorm_g1, mla_w_uq1, mla_kvnorm_g1, mla_w_ukv1, w_out1):
    nb_ctx, seq, d = x_prompt.shape
    nb_lat, dec_seq, _ = x_sample.shape
    st = _Stream(nb_ctx * seq, seq, nb_lat * dec_seq, dec_seq)
    assert st.mp % dec_seq == 0 and dec_seq % seq == 0

    nrow = -(-(1 + nb_lat) // SUBLANES) * SUBLANES
    cvec = jnp.concatenate([c_ctx[None], c, jnp.zeros((nrow - 1 - nb_lat, d), F32)], axis=0)
    mods = [_modulation(cvec, aw, ab).reshape(nrow * 3 * N_SUB, 1, d)
            for aw, ab in ((ada_w0, ada_b0), (ada_w1, ada_b1))]
    norms = [g.reshape(2 * N_SUB, 1, d) for g in (norm_g0, norm_g1)]
    ffns = [(ffn_w1_0, ffn_w2_0), (ffn_w1_1, ffn_w2_1)]

    x = jnp.concatenate([x_prompt.reshape(st.mp, d), x_sample.reshape(st.ms, d)], axis=0)
    h = _pre(st, x, mods[0], norms[0], 0)
    states = []
    for l in range(2):
        mod3, norm3 = mods[l], norms[l]
        w1, w2 = ffns[l]
        y = _ffn(h, w1, w2, 0)
        x, h = _postpre(st, x, y, mod3, norm3, 0, 0.5, nxt=(mod3, norm3, 1))
        if l == 0:
            y, state = _mixer_ab(st, h, nb_ctx, nb_lat, cache_l0_na_k, cache_l0_na_v, state_l0_lru,
                                 w_in0, na_rpb0, lru_conv_w0, lru_conv_b0, lru_wa0, lru_ba0, lru_wx0, lru_bx0,
                                 lru_lam0, w_out0)
        else:
            y, state = _mixer_cd(st, h, nb_ctx, nb_lat, cache_l1_mla_ckv, cache_l1_mla_kpe,
                                 w_in1, cv_dw_w1, cv_dw_b1, cv_ln_g1, cv_ln_b1, mla_qnorm_g1, mla_w_uq1,
                                 mla_kvnorm_g1, mla_w_ukv1, w_out1)
        states.append(state)
        x, h = _postpre(st, x, y, mod3, norm3, 1, 1.0, nxt=(mod3, norm3, 2))
        y = _ffn(h, w1, w2, 1)
        nxt = (mods[1], norms[1], 0) if l == 0 else None
        x, h = _postpre(st, x, y, mod3, norm3, 2, 0.5, nxt=nxt)

    y_p = x[:st.mp].reshape(nb_ctx, seq, d)
    y_s = x[st.mp:].reshape(nb_lat, dec_seq, d)
    (new_k, new_v, new_s), (new_ckv, new_kpe) = states
    return (y_p, y_s, new_k, new_v, new_s, new_ckv, new_kpe)
```

```python
import functools

import numpy as np
import jax
import jax.numpy as jnp
from jax import lax
from jax.experimental import pallas as pl
from jax.experimental.pallas import tpu as pltpu

F32 = jnp.float32
BF16 = jnp.bfloat16

EPS = 1e-6
NEG_INF = -1e30
GRID_W = 64
LRU_C = 8.0
ROPE_BASE = 10000.0
N_SUB = 3
LANES = 128
SUBLANES = 8
MIB = 2 ** 20


def _cparams(sem, vmem_mib):
    return pltpu.CompilerParams(dimension_semantics=sem, vmem_limit_bytes=int(vmem_mib * MIB))


def _pick(n, pref, mult=LANES):
    best = None
    d = mult
    while d <= min(n, pref):
        if n % d == 0:
            best = d
        d += mult
    return best if best is not None else n


def _silu(x):
    return x * jax.nn.sigmoid(x)


def _rms(x, g):
    return x * lax.rsqrt(jnp.mean(x * x, axis=-1, keepdims=True) + EPS) * g


def _mod_kernel(c_ref, w_ref, b_ref, o_ref):
    s = _silu(c_ref[...]).astype(BF16)
    o_ref[...] = jnp.dot(s, w_ref[...].astype(BF16), preferred_element_type=F32) + b_ref[...]


def _modulation(cvec, ada_w, ada_b):
    r, d = cvec.shape
    n = ada_w.shape[1]
    bn = _pick(n, 512)
    return pl.pallas_call(
        _mod_kernel,
        grid=(n // bn,),
        in_specs=[pl.BlockSpec((r, d), lambda j: (0, 0)),
                  pl.BlockSpec((d, bn), lambda j: (0, j)),
                  pl.BlockSpec((1, bn), lambda j: (0, j))],
        out_specs=pl.BlockSpec((r, bn), lambda j: (0, j)),
        out_shape=jax.ShapeDtypeStruct((r, n), F32),
        compiler_params=_cparams(("arbitrary",), 2 * d * bn * 4 / MIB + 8),
        name="modulation",
    )(cvec, ada_w, ada_b.reshape(1, n))


def _load_x(nsplit, x_refs):
    if nsplit is None:
        return x_refs[0][...]
    return jnp.where(pl.program_id(0) < nsplit, x_refs[0][...], x_refs[1][...])


def _pre_kernel(nsplit, *refs):
    nx = 1 if nsplit is None else 2
    g_ref, sh_ref, sc_ref, h_ref = refs[nx:]
    y = _rms(_load_x(nsplit, refs[:nx]), g_ref[0])
    h_ref[...] = (y * (1.0 + sc_ref[0]) + sh_ref[0]).astype(BF16)


def _postpre_kernel(nk, weight, has_next, nsplit, *refs):
    nx = 1 if nsplit is None else 2
    if has_next:
        y_ref, gate_ref, gpost_ref, sh_ref, sc_ref, gpre_ref, xo_ref, h_ref = refs[nx:]
    else:
        y_ref, gate_ref, gpost_ref, xo_ref = refs[nx:]
    y = y_ref[0]
    for k in range(1, nk):
        y = y + y_ref[k]
    xn = _load_x(nsplit, refs[:nx]) + (weight * gate_ref[0]) * _rms(y, gpost_ref[0])
    xo_ref[...] = xn
    if has_next:
        h_ref[...] = (_rms(xn, gpre_ref[0]) * (1.0 + sc_ref[0]) + sh_ref[0]).astype(BF16)


class _Stream:
    def __init__(self, mp, seq, ms, dec_seq):
        self.mp, self.seq, self.ms, self.dec_seq = mp, seq, ms, dec_seq
        self.m = mp + ms

    def mod_row(self, i, tb):
        pos = i * tb
        return jnp.where(pos < self.mp, 0, 1 + (pos - self.mp) // self.dec_seq)


def _mod_spec(st, tb, d, comp, blk0=0):
    return pl.BlockSpec((1, 1, d), lambda i: (st.mod_row(i + blk0, tb) * (3 * N_SUB) + comp, 0, 0))


def _x_specs(st, x, tb, d, blk0=0):
    if not isinstance(x, tuple):
        return None, [pl.BlockSpec((tb, d), lambda i: (i + blk0, 0))], [x]
    nsplit = st.mp // tb
    return nsplit, [pl.BlockSpec((tb, d), lambda i: (jnp.minimum(i, nsplit - 1), 0)),
                    pl.BlockSpec((tb, d), lambda i: (jnp.maximum(i - nsplit, 0), 0))], list(x)


def _gain_spec(d, k):
    return pl.BlockSpec((1, 1, d), lambda i: (k, 0, 0))


def _pre(st, x, mod3, norm3, j, tb=256):
    m, d = st.m, mod3.shape[2]
    nsplit, x_specs, x_args = _x_specs(st, x, tb, d)
    return pl.pallas_call(
        functools.partial(_pre_kernel, nsplit),
        grid=(m // tb,),
        in_specs=x_specs + [_gain_spec(d, 2 * j),
                            _mod_spec(st, tb, d, 3 * j),
                            _mod_spec(st, tb, d, 3 * j + 1)],
        out_specs=pl.BlockSpec((tb, d), lambda i: (i, 0)),
        out_shape=jax.ShapeDtypeStruct((m, d), BF16),
        compiler_params=_cparams(("arbitrary",), 8 * tb * d * 4 / MIB + 8),
        name="pre",
    )(*x_args, norm3, mod3, mod3)


def _postpre(st, x, y, mod3, norm3, j, weight, nxt=None, tb=256, rows=None):
    nk, _, d = y.shape
    row0, m = (0, st.m) if rows is None else rows
    blk0 = row0 // tb
    nsplit, x_specs, x_args = _x_specs(st, x, tb, d, blk0)
    in_specs = x_specs + [pl.BlockSpec((nk, tb, d), lambda i: (0, i + blk0, 0)),
                          _mod_spec(st, tb, d, 3 * j + 2, blk0),
                          _gain_spec(d, 2 * j + 1)]
    args = x_args + [y, mod3, norm3]
    out_specs = [pl.BlockSpec((tb, d), lambda i: (i, 0))]
    out_shape = [jax.ShapeDtypeStruct((m, d), F32)]
    if nxt is not None:
        nmod3, nnorm3, nj = nxt
        in_specs += [_mod_spec(st, tb, d, 3 * nj), _mod_spec(st, tb, d, 3 * nj + 1), _gain_spec(d, 2 * nj)]
        args += [nmod3, nmod3, nnorm3]
        out_specs.append(pl.BlockSpec((tb, d), lambda i: (i, 0)))
        out_shape.append(jax.ShapeDtypeStruct((m, d), BF16))
    out = pl.pallas_call(
        functools.partial(_postpre_kernel, nk, weight, nxt is not None, nsplit),
        grid=(m // tb,),
        in_specs=in_specs,
        out_specs=out_specs,
        out_shape=out_shape,
        compiler_params=_cparams(("arbitrary",), (2 * (len(x_args) + nk + 1.5) + 3) * tb * d * 4 / MIB + 4),
        name="postpre",
    )(*args)
    return out if nxt is not None else (out[0], None)


def _mm_kernel(out_dtype, k_parts, col_limit, has_acc, *refs):
    x_refs = refs[:len(k_parts)]
    if has_acc:
        w_ref, acc_ref, o_ref, wbf_ref = refs[len(k_parts):]
    else:
        w_ref, o_ref, wbf_ref = refs[len(k_parts):]

    @pl.when(pl.program_id(2) == 0)
    def _():
        w = w_ref[...]
        if col_limit is not None:
            first_col, w_cols = col_limit
            col = lax.broadcasted_iota(jnp.int32, w.shape, 1) + (first_col + pl.program_id(1) * w.shape[1])
            w = jnp.where(col < w_cols, w, 0.0)
        wbf_ref[...] = w.astype(BF16)

    acc = acc_ref[0] if has_acc else None
    k0 = 0
    for x_ref, kp in zip(x_refs, k_parts):
        part = jnp.dot(x_ref[...], wbf_ref[k0:k0 + kp, :], preferred_element_type=F32)
        acc = part if acc is None else acc + part
        k0 += kp
    o_ref[0] = acc.astype(out_dtype)


def _ws_matmul(xs, w3, widx, n_out, *, n_off=0, bm=1024, bn=512, ksplit=(0, 1), acc=None, out_dtype=F32,
               name="matmul"):
    xs = list(xs) if isinstance(xs, (list, tuple)) else [xs]
    m = xs[0].shape[0]
    kpart, nk = ksplit
    k_parts = [x.shape[1] // nk for x in xs]
    bk = sum(k_parts)
    assert nk == 1 or len(xs) == 1
    bm = _pick(m, bm, SUBLANES)
    bn = _pick(n_out, bn)
    assert xs[0].shape[1] % nk == 0 and bk % LANES == 0 and n_off % bn == 0
    joff = n_off // bn
    w_cols = w3.shape[2]
    col_limit = (n_off, w_cols) if n_off + n_out > w_cols else None
    osz = jnp.dtype(out_dtype).itemsize
    n_f32_tiles = 2 + (2 if acc is not None else 0)
    vmem = (2 * bk * bn * 4 + bk * bn * 2 + 2 * bm * bk * 2 + 2 * bm * bn * osz + n_f32_tiles * bm * bn * 4) / MIB + 4
    in_specs = [pl.BlockSpec((bm, kp), lambda kk, j, i: (i, kpart)) for kp in k_parts]
    in_specs.append(pl.BlockSpec((None, bk, bn), lambda kk, j, i: (widx, kpart, j + joff)))
    args = xs + [w3]
    aliases = {}
    if acc is not None:
        assert out_dtype == F32
        in_specs.append(pl.BlockSpec((1, bm, bn), lambda kk, j, i: (0, i, j)))
        aliases = {len(args): 0}
        args.append(acc)
    return pl.pallas_call(
        functools.partial(_mm_kernel, out_dtype, tuple(k_parts), col_limit, acc is not None),
        grid=(1, n_out // bn, m // bm),
        in_specs=in_specs,
        out_specs=pl.BlockSpec((1, bm, bn), lambda kk, j, i: (0, i, j)),
        out_shape=jax.ShapeDtypeStruct((1, m, n_out), out_dtype),
        input_output_aliases=aliases,
        scratch_shapes=[pltpu.VMEM((bk, bn), BF16)],
        compiler_params=_cparams(("arbitrary", "arbitrary", "arbitrary"), vmem),
        name=name,
    )(*args)


def _mm_swiglu_kernel(x_ref, wg_ref, wu_ref, o_ref, wg_bf, wu_bf):
    @pl.when(pl.program_id(1) == 0)
    def _():
        wg_bf[...] = wg_ref[...].astype(BF16)
        wu_bf[...] = wu_ref[...].astype(BF16)

    x = x_ref[...]
    g = jnp.dot(x, wg_bf[...], preferred_element_type=F32)
    u = jnp.dot(x, wu_bf[...], preferred_element_type=F32)
    o_ref[...] = (_silu(g) * u).astype(BF16)


def _ws_matmul_swiglu(x, w3, widx, *, bm=1536, bn=256):
    m, k = x.shape
    f = w3.shape[2] // 2
    bm = _pick(m, bm, SUBLANES)
    bn = _pick(f, bn)
    nj = f // bn
    vmem = (4 * k * bn * 4 + 2 * k * bn * 2 + 2 * bm * k * 2 + 2 * bm * bn * 2 + 3 * bm * bn * 4) / MIB + 6
    return pl.pallas_call(
        _mm_swiglu_kernel,
        grid=(nj, m // bm),
        in_specs=[pl.BlockSpec((bm, k), lambda j, i: (i, 0)),
                  pl.BlockSpec((None, k, bn), lambda j, i: (widx, 0, j)),
                  pl.BlockSpec((None, k, bn), lambda j, i: (widx, 0, j + nj))],
        out_specs=pl.BlockSpec((bm, bn), lambda j, i: (i, j)),
        out_shape=jax.ShapeDtypeStruct((m, f), BF16),
        scratch_shapes=[pltpu.VMEM((k, bn), BF16), pltpu.VMEM((k, bn), BF16)],
        compiler_params=_cparams(("arbitrary", "arbitrary"), vmem),
        name="matmul_swiglu",
    )(x, w3, w3)


def _dot_nt(a, b):
    return lax.dot_general(a, b, (((1,), (1,)), ((), ())), preferred_element_type=F32)


def _softmax_pv(scores, values):
    mx = functools.reduce(jnp.maximum, [jnp.max(s, axis=-1, keepdims=True) for s in scores])
    ps = [jnp.exp(s - mx) for s in scores]
    den = functools.reduce(lambda a, b: a + b, [jnp.sum(p, axis=-1, keepdims=True) for p in ps])
    acc = None
    for p, v in zip(ps, values):
        o = jnp.dot(p.astype(BF16), v, preferred_element_type=F32)
        acc = o if acc is None else acc + o
    return acc / den


def _na_ctx_kernel(nh, hd, q_ref, k_ref, v_ref, o_ref, ko_ref, vo_ref):
    scale = hd ** -0.5
    for h in range(nh):
        sl = slice(h * hd, (h + 1) * hd)
        k = k_ref[:, sl]
        v = v_ref[:, sl]
        ko_ref[0, h] = k
        vo_ref[0, h] = v
        s = _dot_nt(q_ref[:, sl].astype(BF16), k.astype(BF16)) * scale
        o_ref[:, sl] = _softmax_pv([s], [v.astype(BF16)]).astype(BF16)


def _na_ctx_attention(z, nb, seq, nh, hd):
    na = nh * hd
    return pl.pallas_call(
        functools.partial(_na_ctx_kernel, nh, hd),
        grid=(nb,),
        in_specs=[pl.BlockSpec((seq, na), lambda b: (b, 0)),
                  pl.BlockSpec((seq, na), lambda b: (b, 1)),
                  pl.BlockSpec((seq, na), lambda b: (b, 2))],
        out_specs=[pl.BlockSpec((seq, na), lambda b: (b, 0)),
                   pl.BlockSpec((1, nh, seq, hd), lambda b: (b, 0, 0, 0)),
                   pl.BlockSpec((1, nh, seq, hd), lambda b: (b, 0, 0, 0))],
        out_shape=[jax.ShapeDtypeStruct((z.shape[0], na), BF16),
                   jax.ShapeDtypeStruct((nb, nh, seq, hd), F32),
                   jax.ShapeDtypeStruct((nb, nh, seq, hd), F32)],
        compiler_params=_cparams(("arbitrary",), 12 * seq * na * 4 / MIB + 8),
        name="na_ctx_attention",
    )(z, z, z)


def _na_window(rows, kr, r):
    return min(max(r - kr // 2, 0), rows - kr)


def _na_row_groups(rows, kr):
    groups, r = [], 0
    while r < rows:
        rs, r1 = _na_window(rows, kr, r), r + 1
        while r1 < rows and _na_window(rows, kr, r1) == rs:
            r1 += 1
        groups.append((r, r1, rs))
        r = r1
    return groups


def _na_lat_kernel(rows, kr, gw, hd, q_ref, k_ref, v_ref, kc_ref, vc_ref, bias_ref, ok_ref, _, o_ref):
    scale = hd ** -0.5
    kc = kc_ref[0, 0].astype(BF16)
    vc = vc_ref[0, 0].astype(BF16)
    for r0, r1, rs in _na_row_groups(rows, kr):
        nq = (r1 - r0) * gw
        q = q_ref[r0 * gw:r1 * gw, :].astype(BF16)
        k = k_ref[rs * gw:(rs + kr) * gw, :].astype(BF16)
        v = v_ref[rs * gw:(rs + kr) * gw, :].astype(BF16)
        bias = bias_ref[0, r0:r1].reshape(nq, kr * gw)
        s_loc = jnp.where(ok_ref[0:nq, :] > 0.5, _dot_nt(q, k) * scale + bias, NEG_INF)
        s_ctx = _dot_nt(q, kc) * scale
        o_ref[r0 * gw:r1 * gw, :] = _softmax_pv([s_loc, s_ctx], [v, vc]).astype(BF16)


def _na_bias_tables(rpb, rows, gw):
    nh, nr, nc = rpb.shape
    wr, wc = (nr + 1) // 2, (nc + 1) // 2
    kr = min(wr, rows)
    qc = np.arange(gw)[:, None]
    kc = np.arange(gw)[None, :]
    cstart = np.clip(qc - wc // 2, 0, gw - wc)
    ok = (kc >= cstart) & (kc < cstart + wc)
    cidx = np.clip(kc - qc + wc - 1, 0, 2 * wc - 2)
    rpb_col = rpb[:, :, cidx]
    tabs = []
    for r in range(rows):
        rs = _na_window(rows, kr, r)
        ridx = rs + np.arange(kr) - r + wr - 1
        t = rpb_col[:, ridx]
        tabs.append(t.transpose(0, 2, 1, 3).reshape(nh, gw, kr * gw))
    ngrp = max(r1 - r0 for r0, r1, _ in _na_row_groups(rows, kr))
    ok_tab = np.tile(ok.astype(np.float32), (ngrp, kr))
    return jnp.stack(tabs, axis=1), jnp.asarray(ok_tab), kr


def _na_lat_attention(z, row_blk0, nb, t, nh, hd, k_ctx, v_ctx, rpb, buf):
    gw = GRID_W
    rows = t // gw
    bias, ok_tab, kr = _na_bias_tables(rpb, rows, gw)
    past = k_ctx.shape[2]
    return pl.pallas_call(
        functools.partial(_na_lat_kernel, rows, kr, gw, hd),
        grid=(nb, nh),
        in_specs=[pl.BlockSpec((t, hd), lambda b, h: (row_blk0 + b, h)),
                  pl.BlockSpec((t, hd), lambda b, h: (row_blk0 + b, nh + h)),
                  pl.BlockSpec((t, hd), lambda b, h: (row_blk0 + b, 2 * nh + h)),
                  pl.BlockSpec((1, 1, past, hd), lambda b, h: (b, h, 0, 0)),
                  pl.BlockSpec((1, 1, past, hd), lambda b, h: (b, h, 0, 0)),
                  pl.BlockSpec((1, rows, gw, kr * gw), lambda b, h: (h, 0, 0, 0)),
                  pl.BlockSpec(ok_tab.shape, lambda b, h: (0, 0)),
                  pl.BlockSpec(memory_space=pl.ANY)],
        out_specs=pl.BlockSpec((t, hd), lambda b, h: (row_blk0 + b, h)),
        out_shape=jax.ShapeDtypeStruct(buf.shape, buf.dtype),
        input_output_aliases={7: 0},
        compiler_params=_cparams(("arbitrary", "arbitrary"), 32),
        name="na_latent_attention",
    )(z, z, z, k_ctx, v_ctx, bias, ok_tab, buf)


def _sigmoid_tanh(x):
    return 0.5 * jnp.tanh(0.5 * x) + 0.5


def _gelu_tanh(x):
    return 0.5 * x * (1.0 + jnp.tanh(0.7978845608028654 * (x + 0.044715 * (x * x * x))))


def _lru_kernel(t, wc, ntap, has_h0, rc, *refs):
    if has_h0:
        (xr_ref, xg_ref, cw_ref, cb_ref, wa_ref, ba_ref, wx_ref, bx_ref, lam_ref, h0_ref, _,
         rec_ref, xp_s, xc_s, a_s, u_s, h_s) = refs
    else:
        (xr_ref, xg_ref, cw_ref, cb_ref, wa_ref, ba_ref, wx_ref, bx_ref, lam_ref,
         rec_ref, st_ref, xp_s, xc_s, a_s, u_s, h_s) = refs
    pad = SUBLANES
    left = (ntap - 1) // 2
    nblk = wc // LANES
    nchunks = t // rc
    ntiles = t // SUBLANES

    xp_s[0:pad, :] = jnp.zeros((pad, wc), F32)
    xp_s[pad + t:pad + t + pad, :] = jnp.zeros((pad, wc), F32)
    xp_s[pad:pad + t, :] = xr_ref[...]

    for r0 in range(0, t, rc):
        acc = jnp.broadcast_to(cb_ref[...], (rc, wc))
        for i in range(ntap):
            base = r0 + pad - left + i
            acc = acc + cw_ref[i:i + 1, :] * xp_s[base:base + rc, :]
        xc_s[r0:r0 + rc, :] = acc

    row = lax.broadcasted_iota(jnp.int32, (SUBLANES, wc), 0)

    for d in range(2):
        reverse = d == 1
        sp = jax.nn.softplus(-lam_ref[d:d + 1, :])

        def gate_chunk(c, carry):
            r0 = pl.multiple_of(c * rc, rc)
            xc = xc_s[pl.ds(r0, rc), :]
            xcb = xc.astype(BF16)
            zs_a, zs_x = [], []
            for n in range(nblk):
                xb = xcb[:, n * LANES:(n + 1) * LANES]
                zs_a.append(jnp.dot(xb, wa_ref[d, n].astype(BF16), preferred_element_type=F32))
                zs_x.append(jnp.dot(xb, wx_ref[d, n].astype(BF16), preferred_element_type=F32))
            rg = _sigmoid_tanh(jnp.concatenate(zs_a, axis=1) + ba_ref[d:d + 1, :])
            ig = _sigmoid_tanh(jnp.concatenate(zs_x, axis=1) + bx_ref[d:d + 1, :])
            a = jnp.exp((-LRU_C) * rg * sp)
            a_s[pl.ds(r0, rc), :] = a
            u_s[pl.ds(r0, rc), :] = jnp.sqrt(1.0 - a * a) * (ig * xc)
            return carry

        lax.fori_loop(0, nchunks, gate_chunk, 0)

        def scan_tile(i, h_prev):
            tile = (ntiles - 1 - i) if reverse else i
            r0 = pl.multiple_of(tile * SUBLANES, SUBLANES)
            a = a_s[pl.ds(r0, SUBLANES), :]
            u = u_s[pl.ds(r0, SUBLANES), :]
            for s in (1, 2, 4):
                if reverse:
                    keep = row < SUBLANES - s
                    shift = SUBLANES - s
                else:
                    keep = row >= s
                    shift = s
                a_sh = jnp.where(keep, pltpu.roll(a, shift, 0), 1.0)
                u_sh = jnp.where(keep, pltpu.roll(u, shift, 0), 0.0)
                u = u + a * u_sh
                a = a * a_sh
            h = u + a * h_prev
            if reverse:
                h_s[pl.ds(r0, SUBLANES), :] = h_s[pl.ds(r0, SUBLANES), :] + h
                return h[0:1, :]
            h_s[pl.ds(r0, SUBLANES), :] = h
            return h[SUBLANES - 1:SUBLANES, :]

        h_init = h0_ref[0, d:d + 1, :] if has_h0 else jnp.zeros((1, wc), F32)
        h_last = lax.fori_loop(0, ntiles, scan_tile, h_init)
        if not has_h0:
            st_ref[0, d:d + 1, :] = h_last

    def out_chunk(c, carry):
        r0 = pl.multiple_of(c * rc, rc)
        rec_ref[pl.ds(r0, rc), :] = (h_s[pl.ds(r0, rc), :] * _gelu_tanh(xg_ref[pl.ds(r0, rc), :])).astype(BF16)
        return carry

    lax.fori_loop(0, nchunks, out_chunk, 0)


def _lru(z, col0, row_blk0, nb, t, width, cw, cb, wa, ba, wx, bx, lam, h0, buf=None):
    wc = _pick(width, 512)
    ncb = width // wc
    ntap = cw.shape[0]
    nblk = wc // LANES
    has_h0 = h0 is not None
    rc = _pick(t, 128, SUBLANES)
    in_specs = [pl.BlockSpec((t, wc), lambda b, c: (row_blk0 + b, col0 // wc + c)),
                pl.BlockSpec((t, wc), lambda b, c: (row_blk0 + b, (col0 + width) // wc + c)),
                pl.BlockSpec((ntap, wc), lambda b, c: (0, c)),
                pl.BlockSpec((1, wc), lambda b, c: (0, c)),
                pl.BlockSpec((2, nblk, LANES, LANES), lambda b, c: (0, c, 0, 0)),
                pl.BlockSpec((2, wc), lambda b, c: (0, c)),
                pl.BlockSpec((2, nblk, LANES, LANES), lambda b, c: (0, c, 0, 0)),
                pl.BlockSpec((2, wc), lambda b, c: (0, c)),
                pl.BlockSpec((2, wc), lambda b, c: (0, c))]
    args = [z, z, cw, cb.reshape(1, width), wa, ba, wx, bx, lam]
    out_specs = [pl.BlockSpec((t, wc), lambda b, c: (row_blk0 + b, c))]
    out_shape = [jax.ShapeDtypeStruct((z.shape[0], width), BF16)]
    aliases = {}
    if has_h0:
        in_specs += [pl.BlockSpec((1, 2, wc), lambda b, c: (b, 0, c)), pl.BlockSpec(memory_space=pl.ANY)]
        args += [h0, buf]
        aliases = {len(args) - 1: 0}
    else:
        out_specs.append(pl.BlockSpec((1, 2, wc), lambda b, c: (b, 0, c)))
        out_shape.append(jax.ShapeDtypeStruct((nb, 2, width), F32))
    out = pl.pallas_call(
        functools.partial(_lru_kernel, t, wc, ntap, has_h0, rc),
        grid=(nb, ncb),
        in_specs=in_specs,
        out_specs=out_specs,
        out_shape=out_shape,
        input_output_aliases=aliases,
        scratch_shapes=[pltpu.VMEM((t + 2 * SUBLANES, wc), F32)] + [pltpu.VMEM((t, wc), F32)] * 4,
        compiler_params=_cparams(("arbitrary", "arbitrary"), 12 * t * wc * 4 / MIB + 10),
        name="rglru",
    )(*args)
    return out if not has_h0 else (out[0], None)


_HALO = 16


def _conv_module_kernel(st, tb, ntap, cch, rch, v_ref, g_ref, vp_ref, gp_ref, vn_ref, gn_ref,
                        w_ref, b_ref, lg_ref, lb_ref, o_ref, u_s, y_s, wb_s):
    i = pl.program_id(0)
    pos = i * tb
    in_ctx = pos < st.mp
    off = jnp.where(in_ctx, pos % st.seq, (pos - st.mp) % st.dec_seq)
    length = jnp.where(in_ctx, st.seq, st.dec_seq)
    has_prev = (off != 0).astype(F32)
    has_next = (off + tb != length).astype(F32)
    c = v_ref.shape[1]
    left = (ntap - 1) // 2
    nrc = tb // rch
    span = tb + 2 * _HALO - SUBLANES

    u_s[0, 0:_HALO, :] = vp_ref[...] * _sigmoid_tanh(gp_ref[...]) * has_prev
    u_s[0, _HALO + tb:_HALO + tb + _HALO, :] = vn_ref[...] * _sigmoid_tanh(gn_ref[...]) * has_next

    def glu_chunk(ci, carry):
        r0 = pl.multiple_of(ci * rch, rch)
        u_s[0, pl.ds(_HALO + r0, rch), :] = v_ref[pl.ds(r0, rch), :] * _sigmoid_tanh(g_ref[pl.ds(r0, rch), :])
        return carry

    lax.fori_loop(0, nrc, glu_chunk, 0)

    for s in range(1, SUBLANES):
        u_s[s, 0:span, :] = u_s[0, s:s + span, :]

    @pl.when(i == 0)
    def _():
        for k in range(ntap):
            wb_s[k * SUBLANES:(k + 1) * SUBLANES, :] = jnp.broadcast_to(w_ref[k:k + 1, :], (SUBLANES, c))

    nsub = rch // SUBLANES
    for c0 in range(0, c, cch):
        def conv_chunk(ci, carry):
            r0 = pl.multiple_of(ci * rch, rch)
            acc = jnp.broadcast_to(b_ref[:, c0:c0 + cch], (nsub, SUBLANES, cch))
            for k in range(ntap):
                off = _HALO - left + k
                rows = pl.ds(pl.multiple_of(r0 + off // SUBLANES * SUBLANES, SUBLANES), rch)
                u = u_s[off % SUBLANES, rows, c0:c0 + cch].reshape(nsub, SUBLANES, cch)
                acc = acc + wb_s[k * SUBLANES:(k + 1) * SUBLANES, c0:c0 + cch][None] * u
            y_s[pl.ds(r0, rch), c0:c0 + cch] = acc.reshape(rch, cch)
            return carry

        lax.fori_loop(0, nrc, conv_chunk, 0)

    nch = _pick(tb, 128, SUBLANES)

    def norm_chunk(ci, carry):
        r0 = pl.multiple_of(ci * nch, nch)
        y = y_s[pl.ds(r0, nch), :]
        yc = y - jnp.mean(y, axis=-1, keepdims=True)
        var = jnp.mean(yc * yc, axis=-1, keepdims=True)
        yn = yc * lax.rsqrt(var + EPS) * lg_ref[...] + lb_ref[...]
        o_ref[pl.ds(r0, nch), :] = (yn * _sigmoid_tanh(yn)).astype(BF16)
        return carry

    lax.fori_loop(0, tb // nch, norm_chunk, 0)


def _conv_module(st, z, c, w, b, ln_g, ln_b, tb=256):
    m = z.shape[0]
    ntap = w.shape[0]
    assert (ntap - 1) // 2 < _HALO and ntap // 2 < _HALO
    hpb = tb // _HALO
    nhb = m // _HALO
    prev_map = lambda col: (lambda i: (jnp.maximum(i * hpb - 1, 0), col))
    next_map = lambda col: (lambda i: (jnp.minimum((i + 1) * hpb, nhb - 1), col))
    vec = lambda: pl.BlockSpec((1, c), lambda i: (0, 0))
    return pl.pallas_call(
        functools.partial(_conv_module_kernel, st, tb, ntap, _pick(c, 512), 64),
        grid=(m // tb,),
        in_specs=[pl.BlockSpec((tb, c), lambda i: (i, 0)),
                  pl.BlockSpec((tb, c), lambda i: (i, 1)),
                  pl.BlockSpec((_HALO, c), prev_map(0)),
                  pl.BlockSpec((_HALO, c), prev_map(1)),
                  pl.BlockSpec((_HALO, c), next_map(0)),
                  pl.BlockSpec((_HALO, c), next_map(1)),
                  pl.BlockSpec((ntap, c), lambda i: (0, 0)),
                  vec(), vec(), vec()],
        out_specs=pl.BlockSpec((tb, c), lambda i: (i, 0)),
        out_shape=jax.ShapeDtypeStruct((m, c), BF16),
        scratch_shapes=[pltpu.VMEM((SUBLANES, tb + 2 * _HALO, c), F32), pltpu.VMEM((tb, c), F32),
                        pltpu.VMEM((ntap * SUBLANES, c), F32)],
        compiler_params=_cparams(("arbitrary",), (SUBLANES * (tb + 2 * _HALO) + 6 * tb) * c * 4 / MIB + 8),
        name="conv_module",
    )(z, z, z, z, z, z, w, b.reshape(1, c), ln_g.reshape(1, c), ln_b.reshape(1, c))


def _mla_norm_kernel(cq_ref, ckv_ref, gq_ref, gkv_ref, q_ref, kv_ref, kvb_ref):
    q_ref[...] = _rms(cq_ref[...], gq_ref[...]).astype(BF16)
    kv = _rms(ckv_ref[...], gkv_ref[...])
    kv_ref[...] = kv
    kvb_ref[...] = kv.astype(BF16)


def _mla_norm(z, col_q, ql, col_kv, kvl, gq, gkv, tb=256):
    m = z.shape[0]
    return pl.pallas_call(
        _mla_norm_kernel,
        grid=(m // tb,),
        in_specs=[pl.BlockSpec((tb, ql), lambda i: (i, col_q // ql)),
                  pl.BlockSpec((tb, kvl), lambda i: (i, col_kv // kvl)),
                  pl.BlockSpec((1, ql), lambda i: (0, 0)),
                  pl.BlockSpec((1, kvl), lambda i: (0, 0))],
        out_specs=[pl.BlockSpec((tb, ql), lambda i: (i, 0)),
                   pl.BlockSpec((tb, kvl), lambda i: (i, 0)),
                   pl.BlockSpec((tb, kvl), lambda i: (i, 0))],
        out_shape=[jax.ShapeDtypeStruct((m, ql), BF16),
                   jax.ShapeDtypeStruct((m, kvl), F32),
                   jax.ShapeDtypeStruct((m, kvl), BF16)],
        compiler_params=_cparams(("arbitrary",), 16),
        name="mla_norm",
    )(z, z, gq.reshape(1, ql), gkv.reshape(1, kvl))


def _swap_halves(x, quarter):
    n = x.shape[-1]
    lane = lax.broadcasted_iota(jnp.int32, x.shape, x.ndim - 1)
    first = (lane % (2 * quarter)) < quarter
    return jnp.where(first, pltpu.roll(x, n - quarter, x.ndim - 1), pltpu.roll(x, quarter, x.ndim - 1))


def _rope_kernel(quarter, rope, qpe_ref, kpe_ref, cq_ref, sq_ref, ck_ref, sk_ref, qo_ref, ko_ref):
    q = qpe_ref[...]
    qo_ref[...] = (q * cq_ref[...] + _swap_halves(q, quarter) * sq_ref[...]).astype(BF16)
    k = kpe_ref[...]
    kr = k * ck_ref[...] + _swap_halves(k, quarter) * sk_ref[...]
    ko_ref[...] = (kr + pltpu.roll(kr, rope, 1)).astype(BF16)


def _rope_tables(t, rope, nrep):
    tok = jnp.arange(t)
    pos = jnp.stack([tok // GRID_W, tok % GRID_W], axis=-1).astype(F32)
    nf = rope // 4
    inv = ROPE_BASE ** (-jnp.arange(nf, dtype=F32) / nf)
    ang = pos[:, :, None] * inv
    cos, sin = jnp.cos(ang), jnp.sin(ang)
    cos_t = jnp.concatenate([cos[:, 0], cos[:, 0], cos[:, 1], cos[:, 1]], axis=-1)
    sin_t = jnp.concatenate([-sin[:, 0], sin[:, 0], -sin[:, 1], sin[:, 1]], axis=-1)
    return jnp.tile(cos_t, (1, nrep)), jnp.tile(sin_t, (1, nrep))


def _rope(q, col_pe, pe_w, kpe, row0, ms, t, rope, tb=256):
    assert 2 * rope == LANES and kpe.shape[1] == LANES
    cq, sq = _rope_tables(t, rope, pe_w // rope)
    ck, sk = _rope_tables(t, rope, 1)
    zpad = jnp.zeros((t, LANES - rope), F32)
    ck = jnp.concatenate([ck, zpad], axis=1)
    sk = jnp.concatenate([sk, zpad], axis=1)
    rb0 = row0 // tb
    tpb = t // tb
    return pl.pallas_call(
        functools.partial(_rope_kernel, rope // 4, rope),
        grid=(ms // tb,),
        in_specs=[pl.BlockSpec((tb, pe_w), lambda i: (rb0 + i, col_pe // pe_w)),
                  pl.BlockSpec((tb, LANES), lambda i: (rb0 + i, 0)),
                  pl.BlockSpec((tb, pe_w), lambda i: (i % tpb, 0)),
                  pl.BlockSpec((tb, pe_w), lambda i: (i % tpb, 0)),
                  pl.BlockSpec((tb, LANES), lambda i: (i % tpb, 0)),
                  pl.BlockSpec((tb, LANES), lambda i: (i % tpb, 0))],
        out_specs=[pl.BlockSpec((tb, pe_w), lambda i: (i, 0)),
                   pl.BlockSpec((tb, LANES), lambda i: (i, 0))],
        out_shape=[jax.ShapeDtypeStruct((ms, pe_w), BF16),
                   jax.ShapeDtypeStruct((ms, LANES), BF16)],
        compiler_params=_cparams(("arbitrary",), 24),
        name="rope",
    )(q, kpe, cq, sq, ck, sk)


def _mla_ctx_kernel(nh, dn, dv, rope, qn_ref, qpe_ref, kv_ref, kpe_ref, o_ref):
    scale = (dn + rope) ** -0.5
    kpe = kpe_ref[...]
    kpe2 = (kpe + pltpu.roll(kpe, rope, 1)).astype(BF16)
    lane = lax.broadcasted_iota(jnp.int32, (qpe_ref.shape[0], LANES), 1)
    for h in range(nh):
        qn = qn_ref[:, h * dn:(h + 1) * dn].astype(BF16)
        kn = kv_ref[:, h * (dn + dv):h * (dn + dv) + dn]
        v = kv_ref[:, h * (dn + dv) + dn:(h + 1) * (dn + dv)]
        blk = (h * rope) // LANES
        half = (h * rope) % LANES // rope
        qp = qpe_ref[:, blk * LANES:(blk + 1) * LANES]
        qp = jnp.where(lane // rope == half, qp, 0.0).astype(BF16)
        s = (_dot_nt(qn, kn) + _dot_nt(qp, kpe2)) * scale
        o_ref[:, h * dv:(h + 1) * dv] = _softmax_pv([s], [v]).astype(BF16)


def _mla_ctx_attention(q, kv, kpe, nb, seq, nh, dn, dv, rope):
    assert dn == dv
    return pl.pallas_call(
        functools.partial(_mla_ctx_kernel, nh, dn, dv, rope),
        grid=(nb,),
        in_specs=[pl.BlockSpec((seq, nh * dn), lambda b: (b, 0)),
                  pl.BlockSpec((seq, nh * rope), lambda b: (b, dn // rope)),
                  pl.BlockSpec((seq, nh * (dn + dv)), lambda b: (b, 0)),
                  pl.BlockSpec((seq, LANES), lambda b: (b, 0))],
        out_specs=pl.BlockSpec((seq, nh * dv), lambda b: (b, 0)),
        out_shape=jax.ShapeDtypeStruct((q.shape[0], nh * dv), BF16),
        compiler_params=_cparams(("arbitrary",), 32),
        name="mla_ctx_attention",
    )(q, q, kv, kpe)


def _mla_lat_kernel(t, qc, dn, dv, rope, qn_ref, qpe_ref, kv_ref, kvc_ref, kpe_ref, kpec_ref, _, o_ref):
    scale = (dn + rope) ** -0.5
    hps = LANES // rope
    kpe = kpe_ref[...]
    kpec = kpec_ref[0]
    lane = lax.broadcasted_iota(jnp.int32, (qc, LANES), 1)
    for hh in range(hps):
        c0 = hh * (dn + dv)
        kn, v = kv_ref[:, c0:c0 + dn], kv_ref[:, c0 + dn:c0 + dn + dv]
        knc, vc = kvc_ref[:, c0:c0 + dn], kvc_ref[:, c0 + dn:c0 + dn + dv]
        for r0 in range(0, t, qc):
            qn = qn_ref[r0:r0 + qc, hh * dn:(hh + 1) * dn].astype(BF16)
            qp = jnp.where(lane // rope == hh, qpe_ref[r0:r0 + qc, :], jnp.zeros((), BF16))
            s_own = (_dot_nt(qn, kn) + _dot_nt(qp, kpe)) * scale
            s_ctx = (_dot_nt(qn, knc) + _dot_nt(qp, kpec)) * scale
            o_ref[r0:r0 + qc, hh * dv:(hh + 1) * dv] = _softmax_pv([s_own, s_ctx], [v, vc]).astype(BF16)


def _mla_lat_attention(q, row_blk0, qpe, kv, kv_ctx, kpe, kpe_ctx, nb, t, nh, dn, dv, rope, buf):
    assert dn == dv == LANES
    hps = LANES // rope
    past = kv_ctx.shape[0] // nb
    return pl.pallas_call(
        functools.partial(_mla_lat_kernel, t, _pick(t, 256, SUBLANES), dn, dv, rope),
        grid=(nb, nh // hps),
        in_specs=[pl.BlockSpec((t, hps * dn), lambda b, j: (row_blk0 + b, j)),
                  pl.BlockSpec((t, LANES), lambda b, j: (b, j)),
                  pl.BlockSpec((t, hps * (dn + dv)), lambda b, j: (row_blk0 + b, j)),
                  pl.BlockSpec((past, hps * (dn + dv)), lambda b, j: (b, j)),
                  pl.BlockSpec((t, LANES), lambda b, j: (b, 0)),
                  pl.BlockSpec((1, past, LANES), lambda b, j: (b, 0, 0)),
                  pl.BlockSpec(memory_space=pl.ANY)],
        out_specs=pl.BlockSpec((t, hps * dv), lambda b, j: (row_blk0 + b, j)),
        out_shape=jax.ShapeDtypeStruct(buf.shape, buf.dtype),
        input_output_aliases={6: 0},
        compiler_params=_cparams(("arbitrary", "arbitrary"), 32),
        name="mla_latent_attention",
    )(q, qpe, kv, kv_ctx, kpe, kpe_ctx, buf)


def _mixer_ab(st, h, nb_ctx, nb_lat, k_ctx, v_ctx, s0, w_in, rpb, conv_w, conv_b, wa, ba, wx, bx, lam, w_out):
    nh, hd = k_ctx.shape[1], k_ctx.shape[3]
    na = nh * hd
    width = lam.shape[1]
    z = _ws_matmul(h, w_in[None], 0, w_in.shape[1], name="w_in0")[0]
    attn, new_k, new_v = _na_ctx_attention(z, nb_ctx, st.seq, nh, hd)
    attn = _na_lat_attention(z, st.mp // st.dec_seq, nb_lat, st.dec_seq, nh, hd, k_ctx, v_ctx, rpb, attn)
    rec, new_s = _lru(z, 3 * na, 0, nb_ctx, st.seq, width, conv_w, conv_b, wa, ba, wx, bx, lam, None)
    rec, _ = _lru(z, 3 * na, st.mp // st.dec_seq, nb_lat, st.dec_seq, width,
                  conv_w, conv_b, wa, ba, wx, bx, lam, s0, rec)
    y = _ws_matmul([attn, rec], w_out[None], 0, w_out.shape[1], name="w_out0")
    return y, (new_k, new_v, new_s)


def _mixer_cd(st, h, nb_ctx, nb_lat, ckv_ctx, kpe_ctx, w_in, dw_w, dw_b, ln_g, ln_b, qn_g, w_uq, kvn_g,
              w_ukv, w_out):
    cw = dw_w.shape[1]
    ql = qn_g.shape[0]
    kvl = kvn_g.shape[0]
    rope = kpe_ctx.shape[2]
    dn = dv = LANES
    nh = w_ukv.shape[1] // (dn + dv)
    n_main = 2 * cw + ql + kvl
    z = _ws_matmul(h, w_in[None], 0, n_main, name="w_in1")[0]
    kpe = _ws_matmul(h, w_in[None], 0, LANES, n_off=n_main, bn=LANES, name="w_kpe")[0]

    u = _conv_module(st, z, cw, dw_w, dw_b, ln_g, ln_b)

    cq_n, ckv_n, ckv_nb = _mla_norm(z, 2 * cw, ql, 2 * cw + ql, kvl, qn_g, kvn_g)
    w_uq3 = w_uq.reshape(ql, nh, dn + rope)
    w_uq_perm = jnp.concatenate([w_uq3[:, :, :dn].reshape(ql, nh * dn),
                                 w_uq3[:, :, dn:].reshape(ql, nh * rope)], axis=1)
    q = _ws_matmul(cq_n, w_uq_perm[None], 0, w_uq_perm.shape[1], bn=1024, name="w_uq")[0]
    kv = _ws_matmul(ckv_nb, w_ukv[None], 0, w_ukv.shape[1], bn=1024, out_dtype=BF16, name="w_ukv")[0]
    past = ckv_ctx.shape[1]
    kv_ctx = _ws_matmul(ckv_ctx.reshape(nb_lat * past, kvl).astype(BF16), w_ukv[None], 0, w_ukv.shape[1],
                        bn=1024, out_dtype=BF16, name="w_ukv_ctx")[0]
    qpe_l, kpe_l = _rope(q, nh * dn, nh * rope, kpe, st.mp, st.ms, st.dec_seq, rope)
    kpe_c2 = jnp.concatenate([kpe_ctx, kpe_ctx], axis=-1).astype(BF16)
    o = _mla_ctx_attention(q, kv, kpe, nb_ctx, st.seq, nh, dn, dv, rope)
    o = _mla_lat_attention(q, st.mp // st.dec_seq, qpe_l, kv, kv_ctx, kpe_l, kpe_c2,
                           nb_lat, st.dec_seq, nh, dn, dv, rope, o)
    y = _ws_matmul([u, o], w_out[None], 0, w_out.shape[1], name="w_out1")
    new_ckv = ckv_n[:st.mp].reshape(nb_ctx, st.seq, kvl)
    new_kpe = kpe[:st.mp, :rope].reshape(nb_ctx, st.seq, rope)
    return y, (new_ckv, new_kpe)


def _ffn(h, w1, w2, f):
    a = _ws_matmul_swiglu(h, w1, f)
    y = _ws_matmul(a, w2, f, w2.shape[2], bm=768, bn=512, ksplit=(0, 2), name="ffn_w2a")
    return _ws_matmul(a, w2, f, w2.shape[2], bm=768, bn=512, ksplit=(1, 2), acc=y, name="ffn_w2b")


def kernel(x_prompt, x_sample, c, cache_l0_na_k, cache_l0_na_v, state_l0_lru, cache_l1_mla_ckv, cache_l1_mla_kpe, c_ctx, ada_w0, ada_b0, norm_g0, ffn_w1_0, ffn_w2_0, w_in0, na_rpb0, lru_conv_w0, lru_conv_b0, lru_wa0, lru_ba0, lru_wx0, lru_bx0, lru_lam0, w_out0, ada_w1, ada_b1, norm_g1, ffn_w1_1, ffn_w2_1, w_in1, cv_dw_w1, cv_dw_b1, cv_ln_g1, cv_ln_b1, mla_qnorm_g1, mla_w_uq1, mla_kvnorm_g1, mla_w_ukv1, w_out1):
    nb_ctx, seq, d = x_prompt.shape
    nb_lat, dec_seq, _ = x_sample.shape
    st = _Stream(nb_ctx * seq, seq, nb_lat * dec_seq, dec_seq)
    assert st.mp % dec_seq == 0 and dec_seq % seq == 0

    nrow = -(-(1 + nb_lat) // SUBLANES) * SUBLANES
    cvec = jnp.concatenate([c_ctx[None], c, jnp.zeros((nrow - 1 - nb_lat, d), F32)], axis=0)
    mods = [_modulation(cvec, aw, ab).reshape(nrow * 3 * N_SUB, 1, d)
            for aw, ab in ((ada_w0, ada_b0), (ada_w1, ada_b1))]
    norms = [g.reshape(2 * N_SUB, 1, d) for g in (norm_g0, norm_g1)]
    ffns = [(ffn_w1_0, ffn_w2_0), (ffn_w1_1, ffn_w2_1)]

    x = (x_prompt.reshape(st.mp, d), x_sample.reshape(st.ms, d))
    h = _pre(st, x, mods[0], norms[0], 0)
    states = []
    for l in range(2):
        mod3, norm3 = mods[l], norms[l]
        w1, w2 = ffns[l]
        y = _ffn(h, w1, w2, 0)
        x, h = _postpre(st, x, y, mod3, norm3, 0, 0.5, nxt=(mod3, norm3, 1))
        if l == 0:
            y, state = _mixer_ab(st, h, nb_ctx, nb_lat, cache_l0_na_k, cache_l0_na_v, state_l0_lru,
                                 w_in0, na_rpb0, lru_conv_w0, lru_conv_b0, lru_wa0, lru_ba0, lru_wx0, lru_bx0,
                                 lru_lam0, w_out0)
        else:
            y, state = _mixer_cd(st, h, nb_ctx, nb_lat, cache_l1_mla_ckv, cache_l1_mla_kpe,
                                 w_in1, cv_dw_w1, cv_dw_b1, cv_ln_g1, cv_ln_b1, mla_qnorm_g1, mla_w_uq1,
                                 mla_kvnorm_g1, mla_w_ukv1, w_out1)
        states.append(state)
        x, h = _postpre(st, x, y, mod3, norm3, 1, 1.0, nxt=(mod3, norm3, 2))
        y = _ffn(h, w1, w2, 1)
        if l == 0:
            x, h = _postpre(st, x, y, mod3, norm3, 2, 0.5, nxt=(mods[1], norms[1], 0))

    y_p, _ = _postpre(st, x, y, mod3, norm3, 2, 0.5, rows=(0, st.mp))
    y_s, _ = _postpre(st, x, y, mod3, norm3, 2, 0.5, rows=(st.mp, st.ms))
    (new_k, new_v, new_s), (new_ckv, new_kpe) = states
    return (y_p.reshape(nb_ctx, seq, d), y_s.reshape(nb_lat, dec_seq, d), new_k, new_v, new_s, new_ckv, new_kpe)
```

```python
import functools

import numpy as np
import jax
import jax.numpy as jnp
from jax import lax
from jax.experimental import pallas as pl
from jax.experimental.pallas import tpu as pltpu

F32 = jnp.float32
BF16 = jnp.bfloat16

EPS = 1e-6
NEG_INF = -1e30
GRID_W = 64
LRU_C = 8.0
ROPE_BASE = 10000.0
N_SUB = 3
LANES = 128
SUBLANES = 8
MIB = 2 ** 20


def _cparams(sem, vmem_mib):
    return pltpu.CompilerParams(dimension_semantics=sem, vmem_limit_bytes=int(vmem_mib * MIB))


def _pick(n, pref, mult=LANES):
    best = None
    d = mult
    while d <= min(n, pref):
        if n % d == 0:
            best = d
        d += mult
    return best if best is not None else n


def _silu(x):
    return x * jax.nn.sigmoid(x)


def _rms(x, g):
    return x * lax.rsqrt(jnp.mean(x * x, axis=-1, keepdims=True) + EPS) * g


def _mod_kernel(c_ref, w_ref, b_ref, o_ref):
    s = _silu(c_ref[...]).astype(BF16)
    o_ref[...] = jnp.dot(s, w_ref[...].astype(BF16), preferred_element_type=F32) + b_ref[...]


def _modulation(cvec, ada_w, ada_b):
    r, d = cvec.shape
    n = ada_w.shape[1]
    bn = _pick(n, 512)
    return pl.pallas_call(
        _mod_kernel,
        grid=(n // bn,),
        in_specs=[pl.BlockSpec((r, d), lambda j: (0, 0)),
                  pl.BlockSpec((d, bn), lambda j: (0, j)),
                  pl.BlockSpec((1, bn), lambda j: (0, j))],
        out_specs=pl.BlockSpec((r, bn), lambda j: (0, j)),
        out_shape=jax.ShapeDtypeStruct((r, n), F32),
        compiler_params=_cparams(("arbitrary",), 2 * d * bn * 4 / MIB + 8),
        name="modulation",
    )(cvec, ada_w, ada_b.reshape(1, n))


def _load_x(nsplit, x_refs):
    if nsplit is None:
        return x_refs[0][...]
    return jnp.where(pl.program_id(0) < nsplit, x_refs[0][...], x_refs[1][...])


def _pre_kernel(nsplit, *refs):
    nx = 1 if nsplit is None else 2
    g_ref, sh_ref, sc_ref, h_ref = refs[nx:]
    y = _rms(_load_x(nsplit, refs[:nx]), g_ref[0])
    h_ref[...] = (y * (1.0 + sc_ref[0]) + sh_ref[0]).astype(BF16)


def _postpre_kernel(nk, weight, has_next, nsplit, *refs):
    nx = 1 if nsplit is None else 2
    if has_next:
        y_ref, gate_ref, gpost_ref, sh_ref, sc_ref, gpre_ref, xo_ref, h_ref = refs[nx:]
    else:
        y_ref, gate_ref, gpost_ref, xo_ref = refs[nx:]
    y = y_ref[0]
    for k in range(1, nk):
        y = y + y_ref[k]
    xn = _load_x(nsplit, refs[:nx]) + (weight * gate_ref[0]) * _rms(y, gpost_ref[0])
    xo_ref[...] = xn
    if has_next:
        h_ref[...] = (_rms(xn, gpre_ref[0]) * (1.0 + sc_ref[0]) + sh_ref[0]).astype(BF16)


class _Stream:
    def __init__(self, mp, seq, ms, dec_seq):
        self.mp, self.seq, self.ms, self.dec_seq = mp, seq, ms, dec_seq
        self.m = mp + ms

    def mod_row(self, i, tb):
        pos = i * tb
        return jnp.where(pos < self.mp, 0, 1 + (pos - self.mp) // self.dec_seq)


def _mod_spec(st, tb, d, comp, blk0=0):
    return pl.BlockSpec((1, 1, d), lambda i: (st.mod_row(i + blk0, tb) * (3 * N_SUB) + comp, 0, 0))


def _x_specs(st, x, tb, d, blk0=0):
    if not isinstance(x, tuple):
        return None, [pl.BlockSpec((tb, d), lambda i: (i + blk0, 0))], [x]
    nsplit = st.mp // tb
    return nsplit, [pl.BlockSpec((tb, d), lambda i: (jnp.minimum(i, nsplit - 1), 0)),
                    pl.BlockSpec((tb, d), lambda i: (jnp.maximum(i - nsplit, 0), 0))], list(x)


def _gain_spec(d, k):
    return pl.BlockSpec((1, 1, d), lambda i: (k, 0, 0))


def _pre(st, x, mod3, norm3, j, tb=256):
    m, d = st.m, mod3.shape[2]
    nsplit, x_specs, x_args = _x_specs(st, x, tb, d)
    return pl.pallas_call(
        functools.partial(_pre_kernel, nsplit),
        grid=(m // tb,),
        in_specs=x_specs + [_gain_spec(d, 2 * j),
                            _mod_spec(st, tb, d, 3 * j),
                            _mod_spec(st, tb, d, 3 * j + 1)],
        out_specs=pl.BlockSpec((tb, d), lambda i: (i, 0)),
        out_shape=jax.ShapeDtypeStruct((m, d), BF16),
        compiler_params=_cparams(("arbitrary",), 8 * tb * d * 4 / MIB + 8),
        name="pre",
    )(*x_args, norm3, mod3, mod3)


def _postpre(st, x, y, mod3, norm3, j, weight, nxt=None, tb=256, rows=None):
    nk, _, d = y.shape
    row0, m = (0, st.m) if rows is None else rows
    blk0 = row0 // tb
    nsplit, x_specs, x_args = _x_specs(st, x, tb, d, blk0)
    in_specs = x_specs + [pl.BlockSpec((nk, tb, d), lambda i: (0, i + blk0, 0)),
                          _mod_spec(st, tb, d, 3 * j + 2, blk0),
                          _gain_spec(d, 2 * j + 1)]
    args = x_args + [y, mod3, norm3]
    out_specs = [pl.BlockSpec((tb, d), lambda i: (i, 0))]
    out_shape = [jax.ShapeDtypeStruct((m, d), F32)]
    if nxt is not None:
        nmod3, nnorm3, nj = nxt
        in_specs += [_mod_spec(st, tb, d, 3 * nj), _mod_spec(st, tb, d, 3 * nj + 1), _gain_spec(d, 2 * nj)]
        args += [nmod3, nmod3, nnorm3]
        out_specs.append(pl.BlockSpec((tb, d), lambda i: (i, 0)))
        out_shape.append(jax.ShapeDtypeStruct((m, d), BF16))
    out = pl.pallas_call(
        functools.partial(_postpre_kernel, nk, weight, nxt is not None, nsplit),
        grid=(m // tb,),
        in_specs=in_specs,
        out_specs=out_specs,
        out_shape=out_shape,
        compiler_params=_cparams(("arbitrary",), (2 * (len(x_args) + nk + 1.5) + 3) * tb * d * 4 / MIB + 4),
        name="postpre",
    )(*args)
    return out if nxt is not None else (out[0], None)


class _MMConfig:
    def __init__(self, **kw):
        self.__dict__.update(kw)


def _mm_step(cfg, s):
    u = jnp.maximum(s - cfg.nq, 0)
    j = u // cfg.ni
    return u - j * cfg.ni, j


def _mm_kernel(cfg, *refs):
    nx, nviews = len(cfg.k_parts), len(cfg.n_offs)
    x_refs, w_refs = refs[:nx], refs[nx:nx + nviews]
    if cfg.has_acc:
        acc_ref, o_ref, wbf_ref = refs[nx + nviews:]
    else:
        o_ref, wbf_ref = refs[nx + nviews:]
    s = pl.program_id(0)
    i, j = _mm_step(cfg, s)
    prologue = s < cfg.nq
    stage_q = jnp.where(prologue, s, i)
    stage_j = jnp.where(prologue, 0, j + 1)
    do_stage = jnp.logical_or(prologue, jnp.logical_and(i < cfg.nq, j + 1 < cfg.nj))

    @pl.when(do_stage)
    def _():
        row0 = pl.multiple_of(stage_q * cfg.bkq, cfg.bkq)
        for v in range(nviews):
            w = w_refs[v][...]
            if cfg.w_cols is not None:
                col = lax.broadcasted_iota(jnp.int32, w.shape, 1) + (cfg.n_offs[v] + stage_j * cfg.bn)
                w = jnp.where(col < cfg.w_cols, w, 0.0)
            wbf_ref[v, stage_j % 2, pl.ds(row0, cfg.bkq), :] = w.astype(BF16)

    @pl.when(jnp.logical_not(prologue))
    def _():
        slot = j % 2
        outs = []
        for v in range(nviews):
            acc = acc_ref[0] if cfg.has_acc else None
            k0 = 0
            for x_ref, kp in zip(x_refs, cfg.k_parts):
                part = jnp.dot(x_ref[...], wbf_ref[v, slot, k0:k0 + kp, :], preferred_element_type=F32)
                acc = part if acc is None else acc + part
                k0 += kp
            outs.append(acc)
        if cfg.swiglu:
            o_ref[0] = (_silu(outs[0]) * outs[1]).astype(o_ref.dtype)
        else:
            o_ref[0] = outs[0].astype(o_ref.dtype)


def _ws_matmul(xs, w3, widx, n_out, *, n_off=0, bm=1536, bn=512, ksplit=(0, 1), acc=None, out_dtype=F32,
               swiglu=False, name="matmul"):
    xs = list(xs) if isinstance(xs, (list, tuple)) else [xs]
    m = xs[0].shape[0]
    kpart, nks = ksplit
    k_parts = tuple(x.shape[1] // nks for x in xs)
    bk = sum(k_parts)
    assert nks == 1 or len(xs) == 1
    bm = _pick(m, bm, SUBLANES)
    bn = _pick(n_out, bn)
    ni, nj = m // bm, n_out // bn
    pack = 2 * SUBLANES
    nq = max(q for q in (4, 2, 1) if q <= ni and bk % (q * pack) == 0)
    n_offs = (n_off, n_off + n_out) if swiglu else (n_off,)
    assert xs[0].shape[1] % nks == 0 and bk % LANES == 0 and all(o % bn == 0 for o in n_offs)
    w_cols = w3.shape[2]
    cfg = _MMConfig(k_parts=k_parts, n_offs=n_offs, bn=bn, bkq=bk // nq, nq=nq, ni=ni, nj=nj,
                    w_cols=w_cols if n_offs[-1] + n_out > w_cols else None,
                    has_acc=acc is not None, swiglu=swiglu)
    nviews = len(n_offs)
    osz = jnp.dtype(out_dtype).itemsize
    n_f32_tiles = nviews + 1 + (2 if acc is not None else 0)
    vmem = (nviews * (2 * cfg.bkq * bn * 4 + 2 * bk * bn * 2) + 2 * bm * bk * 2 + 2 * bm * bn * osz
            + n_f32_tiles * bm * bn * 4) / MIB + 4

    def x_map(s):
        return (_mm_step(cfg, s)[0], kpart)

    def w_map(joff):
        def index(s):
            i, j = _mm_step(cfg, s)
            q = jnp.where(s < nq, s, jnp.minimum(i, nq - 1))
            jt = jnp.where(s < nq, 0, jnp.minimum(j + 1, nj - 1))
            return (widx, kpart * nq + q, jt + joff)
        return index

    def o_map(s):
        i, j = _mm_step(cfg, s)
        return (0, i, j)

    in_specs = [pl.BlockSpec((bm, kp), x_map) for kp in k_parts]
    in_specs += [pl.BlockSpec((None, cfg.bkq, bn), w_map(o // bn)) for o in n_offs]
    args = xs + [w3] * nviews
    aliases = {}
    if acc is not None:
        assert out_dtype == F32 and not swiglu
        in_specs.append(pl.BlockSpec((1, bm, bn), o_map))
        aliases = {len(args): 0}
        args.append(acc)
    return pl.pallas_call(
        functools.partial(_mm_kernel, cfg),
        grid=(nq + nj * ni,),
        in_specs=in_specs,
        out_specs=pl.BlockSpec((1, bm, bn), o_map),
        out_shape=jax.ShapeDtypeStruct((1, m, n_out), out_dtype),
        input_output_aliases=aliases,
        scratch_shapes=[pltpu.VMEM((nviews, 2, bk, bn), BF16)],
        compiler_params=_cparams(("arbitrary",), vmem),
        name=name,
    )(*args)


def _dot_nt(a, b):
    return lax.dot_general(a, b, (((1,), (1,)), ((), ())), preferred_element_type=F32)


def _softmax_pv(scores, values):
    mx = functools.reduce(jnp.maximum, [jnp.max(s, axis=-1, keepdims=True) for s in scores])
    ps = [jnp.exp(s - mx) for s in scores]
    den = functools.reduce(lambda a, b: a + b, [jnp.sum(p, axis=-1, keepdims=True) for p in ps])
    acc = None
    for p, v in zip(ps, values):
        o = jnp.dot(p.astype(BF16), v, preferred_element_type=F32)
        acc = o if acc is None else acc + o
    return acc / den


def _na_ctx_kernel(nh, hd, q_ref, k_ref, v_ref, o_ref, ko_ref, vo_ref):
    scale = hd ** -0.5
    for h in range(nh):
        sl = slice(h * hd, (h + 1) * hd)
        k = k_ref[:, sl]
        v = v_ref[:, sl]
        ko_ref[0, h] = k
        vo_ref[0, h] = v
        s = _dot_nt(q_ref[:, sl].astype(BF16), k.astype(BF16)) * scale
        o_ref[:, sl] = _softmax_pv([s], [v.astype(BF16)]).astype(BF16)


def _na_ctx_attention(z, nb, seq, nh, hd):
    na = nh * hd
    return pl.pallas_call(
        functools.partial(_na_ctx_kernel, nh, hd),
        grid=(nb,),
        in_specs=[pl.BlockSpec((seq, na), lambda b: (b, 0)),
                  pl.BlockSpec((seq, na), lambda b: (b, 1)),
                  pl.BlockSpec((seq, na), lambda b: (b, 2))],
        out_specs=[pl.BlockSpec((seq, na), lambda b: (b, 0)),
                   pl.BlockSpec((1, nh, seq, hd), lambda b: (b, 0, 0, 0)),
                   pl.BlockSpec((1, nh, seq, hd), lambda b: (b, 0, 0, 0))],
        out_shape=[jax.ShapeDtypeStruct((z.shape[0], na), BF16),
                   jax.ShapeDtypeStruct((nb, nh, seq, hd), F32),
                   jax.ShapeDtypeStruct((nb, nh, seq, hd), F32)],
        compiler_params=_cparams(("arbitrary",), 12 * seq * na * 4 / MIB + 8),
        name="na_ctx_attention",
    )(z, z, z)


def _na_window(rows, kr, r):
    return min(max(r - kr // 2, 0), rows - kr)


def _na_row_groups(rows, kr):
    groups, r = [], 0
    while r < rows:
        rs, r1 = _na_window(rows, kr, r), r + 1
        while r1 < rows and _na_window(rows, kr, r1) == rs:
            r1 += 1
        groups.append((r, r1, rs))
        r = r1
    return groups


def _na_lat_kernel(rows, kr, gw, hd, q_ref, k_ref, v_ref, kc_ref, vc_ref, bias_ref, ok_ref, _, o_ref):
    scale = hd ** -0.5
    kc = kc_ref[0, 0].astype(BF16)
    vc = vc_ref[0, 0].astype(BF16)
    for r0, r1, rs in _na_row_groups(rows, kr):
        nq = (r1 - r0) * gw
        q = q_ref[r0 * gw:r1 * gw, :].astype(BF16)
        k = k_ref[rs * gw:(rs + kr) * gw, :].astype(BF16)
        v = v_ref[rs * gw:(rs + kr) * gw, :].astype(BF16)
        bias = bias_ref[0, r0:r1].reshape(nq, kr * gw)
        s_loc = jnp.where(ok_ref[0:nq, :] > 0.5, _dot_nt(q, k) * scale + bias, NEG_INF)
        s_ctx = _dot_nt(q, kc) * scale
        o_ref[r0 * gw:r1 * gw, :] = _softmax_pv([s_loc, s_ctx], [v, vc]).astype(BF16)


def _na_bias_tables(rpb, rows, gw):
    nh, nr, nc = rpb.shape
    wr, wc = (nr + 1) // 2, (nc + 1) // 2
    kr = min(wr, rows)
    qc = np.arange(gw)[:, None]
    kc = np.arange(gw)[None, :]
    cstart = np.clip(qc - wc // 2, 0, gw - wc)
    ok = (kc >= cstart) & (kc < cstart + wc)
    cidx = np.clip(kc - qc + wc - 1, 0, 2 * wc - 2)
    rpb_col = rpb[:, :, cidx]
    tabs = []
    for r in range(rows):
        rs = _na_window(rows, kr, r)
        ridx = rs + np.arange(kr) - r + wr - 1
        t = rpb_col[:, ridx]
        tabs.append(t.transpose(0, 2, 1, 3).reshape(nh, gw, kr * gw))
    ngrp = max(r1 - r0 for r0, r1, _ in _na_row_groups(rows, kr))
    ok_tab = np.tile(ok.astype(np.float32), (ngrp, kr))
    return jnp.stack(tabs, axis=1), jnp.asarray(ok_tab), kr


def _na_lat_attention(z, row_blk0, nb, t, nh, hd, k_ctx, v_ctx, rpb, buf):
    gw = GRID_W
    rows = t // gw
    bias, ok_tab, kr = _na_bias_tables(rpb, rows, gw)
    past = k_ctx.shape[2]
    return pl.pallas_call(
        functools.partial(_na_lat_kernel, rows, kr, gw, hd),
        grid=(nb, nh),
        in_specs=[pl.BlockSpec((t, hd), lambda b, h: (row_blk0 + b, h)),
                  pl.BlockSpec((t, hd), lambda b, h: (row_blk0 + b, nh + h)),
                  pl.BlockSpec((t, hd), lambda b, h: (row_blk0 + b, 2 * nh + h)),
                  pl.BlockSpec((1, 1, past, hd), lambda b, h: (b, h, 0, 0)),
                  pl.BlockSpec((1, 1, past, hd), lambda b, h: (b, h, 0, 0)),
                  pl.BlockSpec((1, rows, gw, kr * gw), lambda b, h: (h, 0, 0, 0)),
                  pl.BlockSpec(ok_tab.shape, lambda b, h: (0, 0)),
                  pl.BlockSpec(memory_space=pl.ANY)],
        out_specs=pl.BlockSpec((t, hd), lambda b, h: (row_blk0 + b, h)),
        out_shape=jax.ShapeDtypeStruct(buf.shape, buf.dtype),
        input_output_aliases={7: 0},
        compiler_params=_cparams(("arbitrary", "arbitrary"), 32),
        name="na_latent_attention",
    )(z, z, z, k_ctx, v_ctx, bias, ok_tab, buf)


def _sigmoid_tanh(x):
    return 0.5 * jnp.tanh(0.5 * x) + 0.5


def _gelu_tanh(x):
    return 0.5 * x * (1.0 + jnp.tanh(0.7978845608028654 * (x + 0.044715 * (x * x * x))))


def _lru_kernel(t, wc, ntap, has_h0, rc, *refs):
    if has_h0:
        (xr_ref, xg_ref, cw_ref, cb_ref, wa_ref, ba_ref, wx_ref, bx_ref, lam_ref, h0_ref, _,
         rec_ref, xp_s, xc_s, a_s, u_s, h_s) = refs
    else:
        (xr_ref, xg_ref, cw_ref, cb_ref, wa_ref, ba_ref, wx_ref, bx_ref, lam_ref,
         rec_ref, st_ref, xp_s, xc_s, a_s, u_s, h_s) = refs
    pad = SUBLANES
    left = (ntap - 1) // 2
    nblk = wc // LANES
    nchunks = t // rc
    ntiles = t // SUBLANES

    xp_s[0:pad, :] = jnp.zeros((pad, wc), F32)
    xp_s[pad + t:pad + t + pad, :] = jnp.zeros((pad, wc), F32)
    xp_s[pad:pad + t, :] = xr_ref[...]

    for r0 in range(0, t, rc):
        acc = jnp.broadcast_to(cb_ref[...], (rc, wc))
        for i in range(ntap):
            base = r0 + pad - left + i
            acc = acc + cw_ref[i:i + 1, :] * xp_s[base:base + rc, :]
        xc_s[r0:r0 + rc, :] = acc

    row = lax.broadcasted_iota(jnp.int32, (SUBLANES, wc), 0)

    for d in range(2):
        reverse = d == 1
        sp = jax.nn.softplus(-lam_ref[d:d + 1, :])

        def gate_chunk(c, carry):
            r0 = pl.multiple_of(c * rc, rc)
            xc = xc_s[pl.ds(r0, rc), :]
            xcb = xc.astype(BF16)
            zs_a, zs_x = [], []
            for n in range(nblk):
                xb = xcb[:, n * LANES:(n + 1) * LANES]
                zs_a.append(jnp.dot(xb, wa_ref[d, n].astype(BF16), preferred_element_type=F32))
                zs_x.append(jnp.dot(xb, wx_ref[d, n].astype(BF16), preferred_element_type=F32))
            rg = _sigmoid_tanh(jnp.concatenate(zs_a, axis=1) + ba_ref[d:d + 1, :])
            ig = _sigmoid_tanh(jnp.concatenate(zs_x, axis=1) + bx_ref[d:d + 1, :])
            a = jnp.exp((-LRU_C) * rg * sp)
            a_s[pl.ds(r0, rc), :] = a
            u_s[pl.ds(r0, rc), :] = jnp.sqrt(1.0 - a * a) * (ig * xc)
            return carry

        lax.fori_loop(0, nchunks, gate_chunk, 0)

        def scan_tile(i, h_prev):
            tile = (ntiles - 1 - i) if reverse else i
            r0 = pl.multiple_of(tile * SUBLANES, SUBLANES)
            a = a_s[pl.ds(r0, SUBLANES), :]
            u = u_s[pl.ds(r0, SUBLANES), :]
            for s in (1, 2, 4):
                if reverse:
                    keep = row < SUBLANES - s
                    shift = SUBLANES - s
                else:
                    keep = row >= s
                    shift = s
                a_sh = jnp.where(keep, pltpu.roll(a, shift, 0), 1.0)
                u_sh = jnp.where(keep, pltpu.roll(u, shift, 0), 0.0)
                u = u + a * u_sh
                a = a * a_sh
            h = u + a * h_prev
            if reverse:
                h_s[pl.ds(r0, SUBLANES), :] = h_s[pl.ds(r0, SUBLANES), :] + h
                return h[0:1, :]
            h_s[pl.ds(r0, SUBLANES), :] = h
            return h[SUBLANES - 1:SUBLANES, :]

        h_init = h0_ref[0, d:d + 1, :] if has_h0 else jnp.zeros((1, wc), F32)
        h_last = lax.fori_loop(0, ntiles, scan_tile, h_init)
        if not has_h0:
            st_ref[0, d:d + 1, :] = h_last

    def out_chunk(c, carry):
        r0 = pl.multiple_of(c * rc, rc)
        rec_ref[pl.ds(r0, rc), :] = (h_s[pl.ds(r0, rc), :] * _gelu_tanh(xg_ref[pl.ds(r0, rc), :])).astype(BF16)
        return carry

    lax.fori_loop(0, nchunks, out_chunk, 0)


def _lru(z, col0, row_blk0, nb, t, width, cw, cb, wa, ba, wx, bx, lam, h0, buf=None):
    wc = _pick(width, 512)
    ncb = width // wc
    ntap = cw.shape[0]
    nblk = wc // LANES
    has_h0 = h0 is not None
    rc = _pick(t, 128, SUBLANES)
    in_specs = [pl.BlockSpec((t, wc), lambda b, c: (row_blk0 + b, col0 // wc + c)),
                pl.BlockSpec((t, wc), lambda b, c: (row_blk0 + b, (col0 + width) // wc + c)),
                pl.BlockSpec((ntap, wc), lambda b, c: (0, c)),
                pl.BlockSpec((1, wc), lambda b, c: (0, c)),
                pl.BlockSpec((2, nblk, LANES, LANES), lambda b, c: (0, c, 0, 0)),
                pl.BlockSpec((2, wc), lambda b, c: (0, c)),
                pl.BlockSpec((2, nblk, LANES, LANES), lambda b, c: (0, c, 0, 0)),
                pl.BlockSpec((2, wc), lambda b, c: (0, c)),
                pl.BlockSpec((2, wc), lambda b, c: (0, c))]
    args = [z, z, cw, cb.reshape(1, width), wa, ba, wx, bx, lam]
    out_specs = [pl.BlockSpec((t, wc), lambda b, c: (row_blk0 + b, c))]
    out_shape = [jax.ShapeDtypeStruct((z.shape[0], width), BF16)]
    aliases = {}
    if has_h0:
        in_specs += [pl.BlockSpec((1, 2, wc), lambda b, c: (b, 0, c)), pl.BlockSpec(memory_space=pl.ANY)]
        args += [h0, buf]
        aliases = {len(args) - 1: 0}
    else:
        out_specs.append(pl.BlockSpec((1, 2, wc), lambda b, c: (b, 0, c)))
        out_shape.append(jax.ShapeDtypeStruct((nb, 2, width), F32))
    out = pl.pallas_call(
        functools.partial(_lru_kernel, t, wc, ntap, has_h0, rc),
        grid=(nb, ncb),
        in_specs=in_specs,
        out_specs=out_specs,
        out_shape=out_shape,
        input_output_aliases=aliases,
        scratch_shapes=[pltpu.VMEM((t + 2 * SUBLANES, wc), F32)] + [pltpu.VMEM((t, wc), F32)] * 4,
        compiler_params=_cparams(("arbitrary", "arbitrary"), 12 * t * wc * 4 / MIB + 10),
        name="rglru",
    )(*args)
    return out if not has_h0 else (out[0], None)


_HALO = 16


def _conv_module_kernel(st, tb, ntap, cch, rch, v_ref, g_ref, vp_ref, gp_ref, vn_ref, gn_ref,
                        w_ref, b_ref, lg_ref, lb_ref, o_ref, u_s, y_s, wb_s):
    i = pl.program_id(0)
    pos = i * tb
    in_ctx = pos < st.mp
    off = jnp.where(in_ctx, pos % st.seq, (pos - st.mp) % st.dec_seq)
    length = jnp.where(in_ctx, st.seq, st.dec_seq)
    has_prev = (off != 0).astype(F32)
    has_next = (off + tb != length).astype(F32)
    c = v_ref.shape[1]
    left = (ntap - 1) // 2
    nrc = tb // rch
    span = tb + 2 * _HALO - SUBLANES

    u_s[0, 0:_HALO, :] = vp_ref[...] * _sigmoid_tanh(gp_ref[...]) * has_prev
    u_s[0, _HALO + tb:_HALO + tb + _HALO, :] = vn_ref[...] * _sigmoid_tanh(gn_ref[...]) * has_next

    def glu_chunk(ci, carry):
        r0 = pl.multiple_of(ci * rch, rch)
        u_s[0, pl.ds(_HALO + r0, rch), :] = v_ref[pl.ds(r0, rch), :] * _sigmoid_tanh(g_ref[pl.ds(r0, rch), :])
        return carry

    lax.fori_loop(0, nrc, glu_chunk, 0)

    for s in range(1, SUBLANES):
        u_s[s, 0:span, :] = u_s[0, s:s + span, :]

    @pl.when(i == 0)
    def _():
        for k in range(ntap):
            wb_s[k * SUBLANES:(k + 1) * SUBLANES, :] = jnp.broadcast_to(w_ref[k:k + 1, :], (SUBLANES, c))

    nsub = rch // SUBLANES
    for c0 in range(0, c, cch):
        def conv_chunk(ci, carry):
            r0 = pl.multiple_of(ci * rch, rch)
            acc = jnp.broadcast_to(b_ref[:, c0:c0 + cch], (nsub, SUBLANES, cch))
            for k in range(ntap):
                off = _HALO - left + k
                rows = pl.ds(pl.multiple_of(r0 + off // SUBLANES * SUBLANES, SUBLANES), rch)
                u = u_s[off % SUBLANES, rows, c0:c0 + cch].reshape(nsub, SUBLANES, cch)
                acc = acc + wb_s[k * SUBLANES:(k + 1) * SUBLANES, c0:c0 + cch][None] * u
            y_s[pl.ds(r0, rch), c0:c0 + cch] = acc.reshape(rch, cch)
            return carry

        lax.fori_loop(0, nrc, conv_chunk, 0)

    nch = _pick(tb, 128, SUBLANES)

    def norm_chunk(ci, carry):
        r0 = pl.multiple_of(ci * nch, nch)
        y = y_s[pl.ds(r0, nch), :]
        yc = y - jnp.mean(y, axis=-1, keepdims=True)
        var = jnp.mean(yc * yc, axis=-1, keepdims=True)
        yn = yc * lax.rsqrt(var + EPS) * lg_ref[...] + lb_ref[...]
        o_ref[pl.ds(r0, nch), :] = (yn * _sigmoid_tanh(yn)).astype(BF16)
        return carry

    lax.fori_loop(0, tb // nch, norm_chunk, 0)


def _conv_module(st, z, c, w, b, ln_g, ln_b, tb=256):
    m = z.shape[0]
    ntap = w.shape[0]
    assert (ntap - 1) // 2 < _HALO and ntap // 2 < _HALO
    hpb = tb // _HALO
    nhb = m // _HALO
    prev_map = lambda col: (lambda i: (jnp.maximum(i * hpb - 1, 0), col))
    next_map = lambda col: (lambda i: (jnp.minimum((i + 1) * hpb, nhb - 1), col))
    vec = lambda: pl.BlockSpec((1, c), lambda i: (0, 0))
    return pl.pallas_call(
        functools.partial(_conv_module_kernel, st, tb, ntap, _pick(c, 512), 64),
        grid=(m // tb,),
        in_specs=[pl.BlockSpec((tb, c), lambda i: (i, 0)),
                  pl.BlockSpec((tb, c), lambda i: (i, 1)),
                  pl.BlockSpec((_HALO, c), prev_map(0)),
                  pl.BlockSpec((_HALO, c), prev_map(1)),
                  pl.BlockSpec((_HALO, c), next_map(0)),
                  pl.BlockSpec((_HALO, c), next_map(1)),
                  pl.BlockSpec((ntap, c), lambda i: (0, 0)),
                  vec(), vec(), vec()],
        out_specs=pl.BlockSpec((tb, c), lambda i: (i, 0)),
        out_shape=jax.ShapeDtypeStruct((m, c), BF16),
        scratch_shapes=[pltpu.VMEM((SUBLANES, tb + 2 * _HALO, c), F32), pltpu.VMEM((tb, c), F32),
                        pltpu.VMEM((ntap * SUBLANES, c), F32)],
        compiler_params=_cparams(("arbitrary",), (SUBLANES * (tb + 2 * _HALO) + 6 * tb) * c * 4 / MIB + 8),
        name="conv_module",
    )(z, z, z, z, z, z, w, b.reshape(1, c), ln_g.reshape(1, c), ln_b.reshape(1, c))


def _mla_norm_kernel(cq_ref, ckv_ref, gq_ref, gkv_ref, q_ref, kv_ref, kvb_ref):
    q_ref[...] = _rms(cq_ref[...], gq_ref[...]).astype(BF16)
    kv = _rms(ckv_ref[...], gkv_ref[...])
    kv_ref[...] = kv
    kvb_ref[...] = kv.astype(BF16)


def _mla_norm(z, col_q, ql, col_kv, kvl, gq, gkv, tb=256):
    m = z.shape[0]
    return pl.pallas_call(
        _mla_norm_kernel,
        grid=(m // tb,),
        in_specs=[pl.BlockSpec((tb, ql), lambda i: (i, col_q // ql)),
                  pl.BlockSpec((tb, kvl), lambda i: (i, col_kv // kvl)),
                  pl.BlockSpec((1, ql), lambda i: (0, 0)),
                  pl.BlockSpec((1, kvl), lambda i: (0, 0))],
        out_specs=[pl.BlockSpec((tb, ql), lambda i: (i, 0)),
                   pl.BlockSpec((tb, kvl), lambda i: (i, 0)),
                   pl.BlockSpec((tb, kvl), lambda i: (i, 0))],
        out_shape=[jax.ShapeDtypeStruct((m, ql), BF16),
                   jax.ShapeDtypeStruct((m, kvl), F32),
                   jax.ShapeDtypeStruct((m, kvl), BF16)],
        compiler_params=_cparams(("arbitrary",), 16),
        name="mla_norm",
    )(z, z, gq.reshape(1, ql), gkv.reshape(1, kvl))


def _swap_halves(x, quarter):
    n = x.shape[-1]
    lane = lax.broadcasted_iota(jnp.int32, x.shape, x.ndim - 1)
    first = (lane % (2 * quarter)) < quarter
    return jnp.where(first, pltpu.roll(x, n - quarter, x.ndim - 1), pltpu.roll(x, quarter, x.ndim - 1))


def _rope_kernel(quarter, rope, qpe_ref, kpe_ref, cq_ref, sq_ref, ck_ref, sk_ref, qo_ref, ko_ref):
    q = qpe_ref[...]
    qo_ref[...] = (q * cq_ref[...] + _swap_halves(q, quarter) * sq_ref[...]).astype(BF16)
    k = kpe_ref[...]
    kr = k * ck_ref[...] + _swap_halves(k, quarter) * sk_ref[...]
    ko_ref[...] = (kr + pltpu.roll(kr, rope, 1)).astype(BF16)


def _rope_tables(t, rope, nrep):
    tok = jnp.arange(t)
    pos = jnp.stack([tok // GRID_W, tok % GRID_W], axis=-1).astype(F32)
    nf = rope // 4
    inv = ROPE_BASE ** (-jnp.arange(nf, dtype=F32) / nf)
    ang = pos[:, :, None] * inv
    cos, sin = jnp.cos(ang), jnp.sin(ang)
    cos_t = jnp.concatenate([cos[:, 0], cos[:, 0], cos[:, 1], cos[:, 1]], axis=-1)
    sin_t = jnp.concatenate([-sin[:, 0], sin[:, 0], -sin[:, 1], sin[:, 1]], axis=-1)
    return jnp.tile(cos_t, (1, nrep)), jnp.tile(sin_t, (1, nrep))


def _rope(q, col_pe, pe_w, kpe, row0, ms, t, rope, tb=256):
    assert 2 * rope == LANES and kpe.shape[1] == LANES
    cq, sq = _rope_tables(t, rope, pe_w // rope)
    ck, sk = _rope_tables(t, rope, 1)
    zpad = jnp.zeros((t, LANES - rope), F32)
    ck = jnp.concatenate([ck, zpad], axis=1)
    sk = jnp.concatenate([sk, zpad], axis=1)
    rb0 = row0 // tb
    tpb = t // tb
    return pl.pallas_call(
        functools.partial(_rope_kernel, rope // 4, rope),
        grid=(ms // tb,),
        in_specs=[pl.BlockSpec((tb, pe_w), lambda i: (rb0 + i, col_pe // pe_w)),
                  pl.BlockSpec((tb, LANES), lambda i: (rb0 + i, 0)),
                  pl.BlockSpec((tb, pe_w), lambda i: (i % tpb, 0)),
                  pl.BlockSpec((tb, pe_w), lambda i: (i % tpb, 0)),
                  pl.BlockSpec((tb, LANES), lambda i: (i % tpb, 0)),
                  pl.BlockSpec((tb, LANES), lambda i: (i % tpb, 0))],
        out_specs=[pl.BlockSpec((tb, pe_w), lambda i: (i, 0)),
                   pl.BlockSpec((tb, LANES), lambda i: (i, 0))],
        out_shape=[jax.ShapeDtypeStruct((ms, pe_w), BF16),
                   jax.ShapeDtypeStruct((ms, LANES), BF16)],
        compiler_params=_cparams(("arbitrary",), 24),
        name="rope",
    )(q, kpe, cq, sq, ck, sk)


def _mla_ctx_kernel(nh, dn, dv, rope, qn_ref, qpe_ref, kv_ref, kpe_ref, o_ref):
    scale = (dn + rope) ** -0.5
    kpe = kpe_ref[...]
    kpe2 = (kpe + pltpu.roll(kpe, rope, 1)).astype(BF16)
    lane = lax.broadcasted_iota(jnp.int32, (qpe_ref.shape[0], LANES), 1)
    for h in range(nh):
        qn = qn_ref[:, h * dn:(h + 1) * dn].astype(BF16)
        kn = kv_ref[:, h * (dn + dv):h * (dn + dv) + dn]
        v = kv_ref[:, h * (dn + dv) + dn:(h + 1) * (dn + dv)]
        blk = (h * rope) // LANES
        half = (h * rope) % LANES // rope
        qp = qpe_ref[:, blk * LANES:(blk + 1) * LANES]
        qp = jnp.where(lane // rope == half, qp, 0.0).astype(BF16)
        s = _dot_nt(jnp.concatenate([qn, qp], axis=1), jnp.concatenate([kn, kpe2], axis=1)) * scale
        o_ref[:, h * dv:(h + 1) * dv] = _softmax_pv([s], [v]).astype(BF16)


def _mla_ctx_attention(q, kv, kpe, nb, seq, nh, dn, dv, rope):
    assert dn == dv
    return pl.pallas_call(
        functools.partial(_mla_ctx_kernel, nh, dn, dv, rope),
        grid=(nb,),
        in_specs=[pl.BlockSpec((seq, nh * dn), lambda b: (b, 0)),
                  pl.BlockSpec((seq, nh * rope), lambda b: (b, dn // rope)),
                  pl.BlockSpec((seq, nh * (dn + dv)), lambda b: (b, 0)),
                  pl.BlockSpec((seq, LANES), lambda b: (b, 0))],
        out_specs=pl.BlockSpec((seq, nh * dv), lambda b: (b, 0)),
        out_shape=jax.ShapeDtypeStruct((q.shape[0], nh * dv), BF16),
        compiler_params=_cparams(("arbitrary",), 32),
        name="mla_ctx_attention",
    )(q, q, kv, kpe)


def _mla_lat_kernel(t, qc, dn, dv, rope, qn_ref, qpe_ref, kv_ref, kvc_ref, kpe_ref, kpec_ref, _, o_ref):
    scale = (dn + rope) ** -0.5
    hps = LANES // rope
    kpe = kpe_ref[...]
    kpec = kpec_ref[0]
    lane = lax.broadcasted_iota(jnp.int32, (qc, LANES), 1)
    for hh in range(hps):
        c0 = hh * (dn + dv)
        v, vc = kv_ref[:, c0 + dn:c0 + dn + dv], kvc_ref[:, c0 + dn:c0 + dn + dv]
        k_own = jnp.concatenate([kv_ref[:, c0:c0 + dn], kpe], axis=1)
        k_ctx = jnp.concatenate([kvc_ref[:, c0:c0 + dn], kpec], axis=1)
        for r0 in range(0, t, qc):
            qn = qn_ref[r0:r0 + qc, hh * dn:(hh + 1) * dn].astype(BF16)
            qp = jnp.where(lane // rope == hh, qpe_ref[r0:r0 + qc, :], jnp.zeros((), BF16))
            q = jnp.concatenate([qn, qp], axis=1)
            s_own = _dot_nt(q, k_own) * scale
            s_ctx = _dot_nt(q, k_ctx) * scale
            o_ref[r0:r0 + qc, hh * dv:(hh + 1) * dv] = _softmax_pv([s_own, s_ctx], [v, vc]).astype(BF16)


def _mla_lat_attention(q, row_blk0, qpe, kv, kv_ctx, kpe, kpe_ctx, nb, t, nh, dn, dv, rope, buf):
    assert dn == dv == LANES
    hps = LANES // rope
    past = kv_ctx.shape[0] // nb
    return pl.pallas_call(
        functools.partial(_mla_lat_kernel, t, _pick(t, 256, SUBLANES), dn, dv, rope),
        grid=(nb, nh // hps),
        in_specs=[pl.BlockSpec((t, hps * dn), lambda b, j: (row_blk0 + b, j)),
                  pl.BlockSpec((t, LANES), lambda b, j: (b, j)),
                  pl.BlockSpec((t, hps * (dn + dv)), lambda b, j: (row_blk0 + b, j)),
                  pl.BlockSpec((past, hps * (dn + dv)), lambda b, j: (b, j)),
                  pl.BlockSpec((t, LANES), lambda b, j: (b, 0)),
                  pl.BlockSpec((1, past, LANES), lambda b, j: (b, 0, 0)),
                  pl.BlockSpec(memory_space=pl.ANY)],
        out_specs=pl.BlockSpec((t, hps * dv), lambda b, j: (row_blk0 + b, j)),
        out_shape=jax.ShapeDtypeStruct(buf.shape, buf.dtype),
        input_output_aliases={6: 0},
        compiler_params=_cparams(("arbitrary", "arbitrary"), 32),
        name="mla_latent_attention",
    )(q, qpe, kv, kv_ctx, kpe, kpe_ctx, buf)


def _mixer_ab(st, h, nb_ctx, nb_lat, k_ctx, v_ctx, s0, w_in, rpb, conv_w, conv_b, wa, ba, wx, bx, lam, w_out):
    nh, hd = k_ctx.shape[1], k_ctx.shape[3]
    na = nh * hd
    width = lam.shape[1]
    z = _ws_matmul(h, w_in[None], 0, w_in.shape[1], name="w_in0")[0]
    attn, new_k, new_v = _na_ctx_attention(z, nb_ctx, st.seq, nh, hd)
    attn = _na_lat_attention(z, st.mp // st.dec_seq, nb_lat, st.dec_seq, nh, hd, k_ctx, v_ctx, rpb, attn)
    rec, new_s = _lru(z, 3 * na, 0, nb_ctx, st.seq, width, conv_w, conv_b, wa, ba, wx, bx, lam, None)
    rec, _ = _lru(z, 3 * na, st.mp // st.dec_seq, nb_lat, st.dec_seq, width,
                  conv_w, conv_b, wa, ba, wx, bx, lam, s0, rec)
    y = _ws_matmul([attn, rec], w_out[None], 0, w_out.shape[1], name="w_out0")
    return y, (new_k, new_v, new_s)


def _mixer_cd(st, h, nb_ctx, nb_lat, ckv_ctx, kpe_ctx, w_in, dw_w, dw_b, ln_g, ln_b, qn_g, w_uq, kvn_g,
              w_ukv, w_out):
    cw = dw_w.shape[1]
    ql = qn_g.shape[0]
    kvl = kvn_g.shape[0]
    rope = kpe_ctx.shape[2]
    dn = dv = LANES
    nh = w_ukv.shape[1] // (dn + dv)
    n_main = 2 * cw + ql + kvl
    z = _ws_matmul(h, w_in[None], 0, n_main, name="w_in1")[0]
    kpe = _ws_matmul(h, w_in[None], 0, LANES, n_off=n_main, bn=LANES, name="w_kpe")[0]

    u = _conv_module(st, z, cw, dw_w, dw_b, ln_g, ln_b)

    cq_n, ckv_n, ckv_nb = _mla_norm(z, 2 * cw, ql, 2 * cw + ql, kvl, qn_g, kvn_g)
    w_uq3 = w_uq.reshape(ql, nh, dn + rope)
    w_uq_perm = jnp.concatenate([w_uq3[:, :, :dn].reshape(ql, nh * dn),
                                 w_uq3[:, :, dn:].reshape(ql, nh * rope)], axis=1)
    q = _ws_matmul(cq_n, w_uq_perm[None], 0, w_uq_perm.shape[1], bn=1024, name="w_uq")[0]
    kv = _ws_matmul(ckv_nb, w_ukv[None], 0, w_ukv.shape[1], bn=1024, out_dtype=BF16, name="w_ukv")[0]
    past = ckv_ctx.shape[1]
    kv_ctx = _ws_matmul(ckv_ctx.reshape(nb_lat * past, kvl).astype(BF16), w_ukv[None], 0, w_ukv.shape[1],
                        bn=1024, out_dtype=BF16, name="w_ukv_ctx")[0]
    qpe_l, kpe_l = _rope(q, nh * dn, nh * rope, kpe, st.mp, st.ms, st.dec_seq, rope)
    kpe_c2 = jnp.concatenate([kpe_ctx, kpe_ctx], axis=-1).astype(BF16)
    o = _mla_ctx_attention(q, kv, kpe, nb_ctx, st.seq, nh, dn, dv, rope)
    o = _mla_lat_attention(q, st.mp // st.dec_seq, qpe_l, kv, kv_ctx, kpe_l, kpe_c2,
                           nb_lat, st.dec_seq, nh, dn, dv, rope, o)
    y = _ws_matmul([u, o], w_out[None], 0, w_out.shape[1], name="w_out1")
    new_ckv = ckv_n[:st.mp].reshape(nb_ctx, st.seq, kvl)
    new_kpe = kpe[:st.mp, :rope].reshape(nb_ctx, st.seq, rope)
    return y, (new_ckv, new_kpe)


def _ffn(h, w1, w2, f):
    a = _ws_matmul(h, w1, f, w1.shape[2] // 2, bm=2048, bn=256, swiglu=True, out_dtype=BF16,
                   name="matmul_swiglu")[0]
    y = _ws_matmul(a, w2, f, w2.shape[2], bm=1024, bn=512, ksplit=(0, 2), name="ffn_w2a")
    return _ws_matmul(a, w2, f, w2.shape[2], bm=1024, bn=512, ksplit=(1, 2), acc=y, name="ffn_w2b")


def kernel(x_prompt, x_sample, c, cache_l0_na_k, cache_l0_na_v, state_l0_lru, cache_l1_mla_ckv, cache_l1_mla_kpe, c_ctx, ada_w0, ada_b0, norm_g0, ffn_w1_0, ffn_w2_0, w_in0, na_rpb0, lru_conv_w0, lru_conv_b0, lru_wa0, lru_ba0, lru_wx0, lru_bx0, lru_lam0, w_out0, ada_w1, ada_b1, norm_g1, ffn_w1_1, ffn_w2_1, w_in1, cv_dw_w1, cv_dw_b1, cv_ln_g1, cv_ln_b1, mla_qnorm_g1, mla_w_uq1, mla_kvnorm_g1, mla_w_ukv1, w_out1):
    nb_ctx, seq, d = x_prompt.shape
    nb_lat, dec_seq, _ = x_sample.shape
    st = _Stream(nb_ctx * seq, seq, nb_lat * dec_seq, dec_seq)
    assert st.mp % dec_seq == 0 and dec_seq % seq == 0

    nrow = -(-(1 + nb_lat) // SUBLANES) * SUBLANES
    cvec = jnp.concatenate([c_ctx[None], c, jnp.zeros((nrow - 1 - nb_lat, d), F32)], axis=0)
    mods = [_modulation(cvec, aw, ab).reshape(nrow * 3 * N_SUB, 1, d)
            for aw, ab in ((ada_w0, ada_b0), (ada_w1, ada_b1))]
    norms = [g.reshape(2 * N_SUB, 1, d) for g in (norm_g0, norm_g1)]
    ffns = [(ffn_w1_0, ffn_w2_0), (ffn_w1_1, ffn_w2_1)]

    x = (x_prompt.reshape(st.mp, d), x_sample.reshape(st.ms, d))
    h = _pre(st, x, mods[0], norms[0], 0)
    states = []
    for l in range(2):
        mod3, norm3 = mods[l], norms[l]
        w1, w2 = ffns[l]
        y = _ffn(h, w1, w2, 0)
        x, h = _postpre(st, x, y, mod3, norm3, 0, 0.5, nxt=(mod3, norm3, 1))
        if l == 0:
            y, state = _mixer_ab(st, h, nb_ctx, nb_lat, cache_l0_na_k, cache_l0_na_v, state_l0_lru,
                                 w_in0, na_rpb0, lru_conv_w0, lru_conv_b0, lru_wa0, lru_ba0, lru_wx0, lru_bx0,
                                 lru_lam0, w_out0)
        else:
            y, state = _mixer_cd(st, h, nb_ctx, nb_lat, cache_l1_mla_ckv, cache_l1_mla_kpe,
                                 w_in1, cv_dw_w1, cv_dw_b1, cv_ln_g1, cv_ln_b1, mla_qnorm_g1, mla_w_uq1,
                                 mla_kvnorm_g1, mla_w_ukv1, w_out1)
        states.append(state)
        x, h = _postpre(st, x, y, mod3, norm3, 1, 1.0, nxt=(mod3, norm3, 2))
        y = _ffn(h, w1, w2, 1)
        if l == 0:
            x, h = _postpre(st, x, y, mod3, norm3, 2, 0.5, nxt=(mods[1], norms[1], 0))

    y_p, _ = _postpre(st, x, y, mod3, norm3, 2, 0.5, rows=(0, st.mp))
    y_s, _ = _postpre(st, x, y, mod3, norm3, 2, 0.5, rows=(st.mp, st.ms))
    (new_k, new_v, new_s), (new_ckv, new_kpe) = states
    return (y_p.reshape(nb_ctx, seq, d), y_s.reshape(nb_lat, dec_seq, d), new_k, new_v, new_s, new_ckv, new_kpe)
```

```python
import functools

import numpy as np
import jax
import jax.numpy as jnp
from jax import lax
from jax.experimental import pallas as pl
from jax.experimental.pallas import tpu as pltpu

F32 = jnp.float32
BF16 = jnp.bfloat16

EPS = 1e-6
NEG_INF = -1e30
GRID_W = 64
LRU_C = 8.0
ROPE_BASE = 10000.0
N_SUB = 3
LANES = 128
SUBLANES = 8
MIB = 2 ** 20


def _cparams(sem, vmem_mib):
    return pltpu.CompilerParams(dimension_semantics=sem, vmem_limit_bytes=int(vmem_mib * MIB))


def _pick(n, pref, mult=LANES):
    best = None
    d = mult
    while d <= min(n, pref):
        if n % d == 0:
            best = d
        d += mult
    return best if best is not None else n


def _silu(x):
    return x * jax.nn.sigmoid(x)


def _rms(x, g):
    return x * lax.rsqrt(jnp.mean(x * x, axis=-1, keepdims=True) + EPS) * g


def _mod_kernel(c_ref, w_ref, b_ref, o_ref):
    s = _silu(c_ref[...]).astype(BF16)
    o_ref[...] = jnp.dot(s, w_ref[...].astype(BF16), preferred_element_type=F32) + b_ref[...]


def _modulation(cvec, ada_w, ada_b):
    r, d = cvec.shape
    n = ada_w.shape[1]
    bn = _pick(n, 512)
    return pl.pallas_call(
        _mod_kernel,
        grid=(n // bn,),
        in_specs=[pl.BlockSpec((r, d), lambda j: (0, 0)),
                  pl.BlockSpec((d, bn), lambda j: (0, j)),
                  pl.BlockSpec((1, bn), lambda j: (0, j))],
        out_specs=pl.BlockSpec((r, bn), lambda j: (0, j)),
        out_shape=jax.ShapeDtypeStruct((r, n), F32),
        compiler_params=_cparams(("arbitrary",), 2 * d * bn * 4 / MIB + 8),
        name="modulation",
    )(cvec, ada_w, ada_b.reshape(1, n))


def _load_x(nsplit, x_refs):
    if nsplit is None:
        return x_refs[0][...]
    return jnp.where(pl.program_id(0) < nsplit, x_refs[0][...], x_refs[1][...])


def _pre_kernel(nsplit, *refs):
    nx = 1 if nsplit is None else 2
    g_ref, sh_ref, sc_ref, h_ref = refs[nx:]
    y = _rms(_load_x(nsplit, refs[:nx]), g_ref[0])
    h_ref[...] = (y * (1.0 + sc_ref[0]) + sh_ref[0]).astype(BF16)


def _postpre_kernel(nk, weight, has_next, nsplit, *refs):
    nx = 1 if nsplit is None else 2
    if has_next:
        y_ref, gate_ref, gpost_ref, sh_ref, sc_ref, gpre_ref, xo_ref, h_ref = refs[nx:]
    else:
        y_ref, gate_ref, gpost_ref, xo_ref = refs[nx:]
    y = y_ref[0]
    for k in range(1, nk):
        y = y + y_ref[k]
    xn = _load_x(nsplit, refs[:nx]) + (weight * gate_ref[0]) * _rms(y, gpost_ref[0])
    xo_ref[...] = xn
    if has_next:
        h_ref[...] = (_rms(xn, gpre_ref[0]) * (1.0 + sc_ref[0]) + sh_ref[0]).astype(BF16)


class _Stream:
    def __init__(self, mp, seq, ms, dec_seq):
        self.mp, self.seq, self.ms, self.dec_seq = mp, seq, ms, dec_seq
        self.m = mp + ms

    def mod_row(self, i, tb):
        pos = i * tb
        return jnp.where(pos < self.mp, 0, 1 + (pos - self.mp) // self.dec_seq)


def _mod_spec(st, tb, d, comp, blk0=0):
    return pl.BlockSpec((1, 1, d), lambda i: (st.mod_row(i + blk0, tb) * (3 * N_SUB) + comp, 0, 0))


def _x_specs(st, x, tb, d, blk0=0):
    if not isinstance(x, tuple):
        return None, [pl.BlockSpec((tb, d), lambda i: (i + blk0, 0))], [x]
    nsplit = st.mp // tb
    return nsplit, [pl.BlockSpec((tb, d), lambda i: (jnp.minimum(i, nsplit - 1), 0)),
                    pl.BlockSpec((tb, d), lambda i: (jnp.maximum(i - nsplit, 0), 0))], list(x)


def _gain_spec(d, k):
    return pl.BlockSpec((1, 1, d), lambda i: (k, 0, 0))


def _pre(st, x, mod3, norm3, j, tb=256):
    m, d = st.m, mod3.shape[2]
    nsplit, x_specs, x_args = _x_specs(st, x, tb, d)
    return pl.pallas_call(
        functools.partial(_pre_kernel, nsplit),
        grid=(m // tb,),
        in_specs=x_specs + [_gain_spec(d, 2 * j),
                            _mod_spec(st, tb, d, 3 * j),
                            _mod_spec(st, tb, d, 3 * j + 1)],
        out_specs=pl.BlockSpec((tb, d), lambda i: (i, 0)),
        out_shape=jax.ShapeDtypeStruct((m, d), BF16),
        compiler_params=_cparams(("arbitrary",), 8 * tb * d * 4 / MIB + 8),
        name="pre",
    )(*x_args, norm3, mod3, mod3)


def _postpre(st, x, y, mod3, norm3, j, weight, nxt=None, tb=256, rows=None):
    nk, _, d = y.shape
    row0, m = (0, st.m) if rows is None else rows
    blk0 = row0 // tb
    nsplit, x_specs, x_args = _x_specs(st, x, tb, d, blk0)
    in_specs = x_specs + [pl.BlockSpec((nk, tb, d), lambda i: (0, i + blk0, 0)),
                          _mod_spec(st, tb, d, 3 * j + 2, blk0),
                          _gain_spec(d, 2 * j + 1)]
    args = x_args + [y, mod3, norm3]
    out_specs = [pl.BlockSpec((tb, d), lambda i: (i, 0))]
    out_shape = [jax.ShapeDtypeStruct((m, d), F32)]
    if nxt is not None:
        nmod3, nnorm3, nj = nxt
        in_specs += [_mod_spec(st, tb, d, 3 * nj), _mod_spec(st, tb, d, 3 * nj + 1), _gain_spec(d, 2 * nj)]
        args += [nmod3, nmod3, nnorm3]
        out_specs.append(pl.BlockSpec((tb, d), lambda i: (i, 0)))
        out_shape.append(jax.ShapeDtypeStruct((m, d), BF16))
    out = pl.pallas_call(
        functools.partial(_postpre_kernel, nk, weight, nxt is not None, nsplit),
        grid=(m // tb,),
        in_specs=in_specs,
        out_specs=out_specs,
        out_shape=out_shape,
        compiler_params=_cparams(("arbitrary",), (2 * (len(x_args) + nk + 1.5) + 3) * tb * d * 4 / MIB + 4),
        name="postpre",
    )(*args)
    return out if nxt is not None else (out[0], None)


class _MMConfig:
    def __init__(self, **kw):
        self.__dict__.update(kw)


def _mm_step(cfg, s):
    u = jnp.maximum(s - cfg.nq, 0)
    j = u // cfg.ni
    return u - j * cfg.ni, j


def _mm_kernel(cfg, *refs):
    nx, nviews = len(cfg.k_parts), len(cfg.n_offs)
    x_refs, w_refs = refs[:nx], refs[nx:nx + nviews]
    if cfg.has_acc:
        acc_ref, o_ref, wbf_ref = refs[nx + nviews:]
    else:
        o_ref, wbf_ref = refs[nx + nviews:]
    s = pl.program_id(0)
    i, j = _mm_step(cfg, s)
    prologue = s < cfg.nq
    stage_q = jnp.where(prologue, s, i)
    stage_j = jnp.where(prologue, 0, j + 1)
    do_stage = jnp.logical_or(prologue, jnp.logical_and(i < cfg.nq, j + 1 < cfg.nj))

    @pl.when(do_stage)
    def _():
        row0 = pl.multiple_of(stage_q * cfg.bkq, cfg.bkq)
        for v in range(nviews):
            w = w_refs[v][...]
            if cfg.w_cols is not None:
                col = lax.broadcasted_iota(jnp.int32, w.shape, 1) + (cfg.n_offs[v] + stage_j * cfg.bn)
                w = jnp.where(col < cfg.w_cols, w, 0.0)
            wbf_ref[v, stage_j % 2, pl.ds(row0, cfg.bkq), :] = w.astype(BF16)

    @pl.when(jnp.logical_not(prologue))
    def _():
        slot = j % 2
        outs = []
        for v in range(nviews):
            acc = acc_ref[0] if cfg.has_acc else None
            k0 = 0
            for x_ref, kp in zip(x_refs, cfg.k_parts):
                part = jnp.dot(x_ref[...], wbf_ref[v, slot, k0:k0 + kp, :], preferred_element_type=F32)
                acc = part if acc is None else acc + part
                k0 += kp
            outs.append(acc)
        if cfg.swiglu:
            o_ref[0] = (_silu(outs[0]) * outs[1]).astype(o_ref.dtype)
        else:
            o_ref[0] = outs[0].astype(o_ref.dtype)


def _ws_matmul(xs, w3, widx, n_out, *, n_off=0, bm=1536, bn=512, ksplit=(0, 1), acc=None, out_dtype=F32,
               swiglu=False, name="matmul"):
    xs = list(xs) if isinstance(xs, (list, tuple)) else [xs]
    m = xs[0].shape[0]
    kpart, nks = ksplit
    k_parts = tuple(x.shape[1] // nks for x in xs)
    bk = sum(k_parts)
    assert nks == 1 or len(xs) == 1
    bm = _pick(m, bm, SUBLANES)
    bn = _pick(n_out, bn)
    ni, nj = m // bm, n_out // bn
    pack = 2 * SUBLANES
    nq = max(q for q in (4, 2, 1) if q <= ni and bk % (q * pack) == 0)
    n_offs = (n_off, n_off + n_out) if swiglu else (n_off,)
    assert xs[0].shape[1] % nks == 0 and bk % LANES == 0 and all(o % bn == 0 for o in n_offs)
    w_cols = w3.shape[2]
    cfg = _MMConfig(k_parts=k_parts, n_offs=n_offs, bn=bn, bkq=bk // nq, nq=nq, ni=ni, nj=nj,
                    w_cols=w_cols if n_offs[-1] + n_out > w_cols else None,
                    has_acc=acc is not None, swiglu=swiglu)
    nviews = len(n_offs)
    osz = jnp.dtype(out_dtype).itemsize
    n_f32_tiles = nviews + 1 + (2 if acc is not None else 0)
    vmem = (nviews * (2 * cfg.bkq * bn * 4 + 2 * bk * bn * 2) + 2 * bm * bk * 2 + 2 * bm * bn * osz
            + n_f32_tiles * bm * bn * 4) / MIB + 4

    def x_map(s):
        return (_mm_step(cfg, s)[0], kpart)

    def w_map(joff):
        def index(s):
            i, j = _mm_step(cfg, s)
            q = jnp.where(s < nq, s, jnp.minimum(i, nq - 1))
            jt = jnp.where(s < nq, 0, jnp.minimum(j + 1, nj - 1))
            return (widx, kpart * nq + q, jt + joff)
        return index

    def o_map(s):
        i, j = _mm_step(cfg, s)
        return (0, i, j)

    in_specs = [pl.BlockSpec((bm, kp), x_map) for kp in k_parts]
    in_specs += [pl.BlockSpec((None, cfg.bkq, bn), w_map(o // bn)) for o in n_offs]
    args = xs + [w3] * nviews
    aliases = {}
    if acc is not None:
        assert out_dtype == F32 and not swiglu
        in_specs.append(pl.BlockSpec((1, bm, bn), o_map))
        aliases = {len(args): 0}
        args.append(acc)
    return pl.pallas_call(
        functools.partial(_mm_kernel, cfg),
        grid=(nq + nj * ni,),
        in_specs=in_specs,
        out_specs=pl.BlockSpec((1, bm, bn), o_map),
        out_shape=jax.ShapeDtypeStruct((1, m, n_out), out_dtype),
        input_output_aliases=aliases,
        scratch_shapes=[pltpu.VMEM((nviews, 2, bk, bn), BF16)],
        compiler_params=_cparams(("arbitrary",), vmem),
        name=name,
    )(*args)


def _dot_nt(a, b):
    return lax.dot_general(a, b, (((1,), (1,)), ((), ())), preferred_element_type=F32)


def _softmax_pv(scores, values):
    mx = functools.reduce(jnp.maximum, [jnp.max(s, axis=-1, keepdims=True) for s in scores])
    ps = [jnp.exp(s - mx) for s in scores]
    den = functools.reduce(lambda a, b: a + b, [jnp.sum(p, axis=-1, keepdims=True) for p in ps])
    acc = None
    for p, v in zip(ps, values):
        o = jnp.dot(p.astype(BF16), v, preferred_element_type=F32)
        acc = o if acc is None else acc + o
    return acc / den


def _shared_buffer(rows, cols):
    return jnp.zeros((rows, cols), BF16)


def _na_ctx_kernel(nh, hd, q_ref, k_ref, v_ref, _, o_ref, ko_ref, vo_ref):
    scale = hd ** -0.5
    for h in range(nh):
        sl = slice(h * hd, (h + 1) * hd)
        k = k_ref[:, sl]
        v = v_ref[:, sl]
        ko_ref[0, h] = k
        vo_ref[0, h] = v
        s = _dot_nt(q_ref[:, sl].astype(BF16), k.astype(BF16)) * scale
        o_ref[:, sl] = _softmax_pv([s], [v.astype(BF16)]).astype(BF16)


def _na_ctx_attention(z, nb, seq, nh, hd):
    na = nh * hd
    return pl.pallas_call(
        functools.partial(_na_ctx_kernel, nh, hd),
        grid=(nb,),
        in_specs=[pl.BlockSpec((seq, na), lambda b: (b, 0)),
                  pl.BlockSpec((seq, na), lambda b: (b, 1)),
                  pl.BlockSpec((seq, na), lambda b: (b, 2)),
                  pl.BlockSpec(memory_space=pl.ANY)],
        input_output_aliases={3: 0},
        out_specs=[pl.BlockSpec((seq, na), lambda b: (b, 0)),
                   pl.BlockSpec((1, nh, seq, hd), lambda b: (b, 0, 0, 0)),
                   pl.BlockSpec((1, nh, seq, hd), lambda b: (b, 0, 0, 0))],
        out_shape=[jax.ShapeDtypeStruct((z.shape[0], na), BF16),
                   jax.ShapeDtypeStruct((nb, nh, seq, hd), F32),
                   jax.ShapeDtypeStruct((nb, nh, seq, hd), F32)],
        compiler_params=_cparams(("arbitrary",), 12 * seq * na * 4 / MIB + 8),
        name="na_ctx_attention",
    )(z, z, z, _shared_buffer(z.shape[0], na))


def _na_window(rows, kr, r):
    return min(max(r - kr // 2, 0), rows - kr)


def _na_row_groups(rows, kr):
    groups, r = [], 0
    while r < rows:
        rs, r1 = _na_window(rows, kr, r), r + 1
        while r1 < rows and _na_window(rows, kr, r1) == rs:
            r1 += 1
        groups.append((r, r1, rs))
        r = r1
    return groups


def _na_lat_kernel(rows, kr, wr, gw, hd, q_ref, k_ref, v_ref, kc_ref, vc_ref, bias_ref, ok_ref, _, o_ref):
    scale = hd ** -0.5
    nphase = LANES // gw
    kc = kc_ref[0, 0].astype(BF16)
    vc = vc_ref[0, 0].astype(BF16)
    for r0, r1, rs in _na_row_groups(rows, kr):
        nq = (r1 - r0) * gw
        q = q_ref[r0 * gw:r1 * gw, :].astype(BF16)
        k = k_ref[rs * gw:(rs + kr) * gw, :].astype(BF16)
        v = v_ref[rs * gw:(rs + kr) * gw, :].astype(BF16)
        biases = []
        for r in range(r0, r1):
            off = rs - r + wr - 1
            start = (off - off % nphase) * gw
            biases.append(bias_ref[0, off % nphase, :, start:start + kr * gw])
        bias = biases[0] if len(biases) == 1 else jnp.concatenate(biases, axis=0)
        s_loc = jnp.where(ok_ref[0:nq, :] > 0.5, _dot_nt(q, k) * scale + bias, NEG_INF)
        s_ctx = _dot_nt(q, kc) * scale
        o_ref[r0 * gw:r1 * gw, :] = _softmax_pv([s_loc, s_ctx], [v, vc]).astype(BF16)


def _na_bias_tables(rpb, rows, gw):
    nh, nr, nc = rpb.shape
    wr, wc = (nr + 1) // 2, (nc + 1) // 2
    kr = min(wr, rows)
    assert LANES % gw == 0
    qc = np.arange(gw)[:, None]
    kc = np.arange(gw)[None, :]
    cstart = np.clip(qc - wc // 2, 0, gw - wc)
    ok = (kc >= cstart) & (kc < cstart + wc)
    cidx = np.clip(kc - qc + wc - 1, 0, 2 * wc - 2)
    rpb_col = rpb[:, :, cidx]
    base = rpb_col.transpose(0, 2, 1, 3).reshape(nh, gw, nr * gw)
    width = -(-nr * gw // LANES) * LANES
    tab = jnp.stack([jnp.pad(base[:, :, p * gw:], ((0, 0), (0, 0), (0, width - nr * gw + p * gw)))
                     for p in range(LANES // gw)], axis=1)
    ngrp = max(r1 - r0 for r0, r1, _ in _na_row_groups(rows, kr))
    ok_tab = np.tile(ok.astype(np.float32), (ngrp, kr))
    return tab, jnp.asarray(ok_tab), kr, wr


def _na_lat_attention(z, row_blk0, nb, t, nh, hd, k_ctx, v_ctx, rpb, buf):
    gw = GRID_W
    rows = t // gw
    bias, ok_tab, kr, wr = _na_bias_tables(rpb, rows, gw)
    past = k_ctx.shape[2]
    return pl.pallas_call(
        functools.partial(_na_lat_kernel, rows, kr, wr, gw, hd),
        grid=(nb, nh),
        in_specs=[pl.BlockSpec((t, hd), lambda b, h: (row_blk0 + b, h)),
                  pl.BlockSpec((t, hd), lambda b, h: (row_blk0 + b, nh + h)),
                  pl.BlockSpec((t, hd), lambda b, h: (row_blk0 + b, 2 * nh + h)),
                  pl.BlockSpec((1, 1, past, hd), lambda b, h: (b, h, 0, 0)),
                  pl.BlockSpec((1, 1, past, hd), lambda b, h: (b, h, 0, 0)),
                  pl.BlockSpec((1,) + bias.shape[1:], lambda b, h: (h, 0, 0, 0)),
                  pl.BlockSpec(ok_tab.shape, lambda b, h: (0, 0)),
                  pl.BlockSpec(memory_space=pl.ANY)],
        out_specs=pl.BlockSpec((t, hd), lambda b, h: (row_blk0 + b, h)),
        out_shape=jax.ShapeDtypeStruct(buf.shape, buf.dtype),
        input_output_aliases={7: 0},
        compiler_params=_cparams(("arbitrary", "arbitrary"), 32),
        name="na_latent_attention",
    )(z, z, z, k_ctx, v_ctx, bias, ok_tab, buf)


def _sigmoid_tanh(x):
    return 0.5 * jnp.tanh(0.5 * x) + 0.5


def _gelu_tanh(x):
    return 0.5 * x * (1.0 + jnp.tanh(0.7978845608028654 * (x + 0.044715 * (x * x * x))))


def _lru_kernel(t, wc, ntap, has_h0, rc, *refs):
    if has_h0:
        (xr_ref, xg_ref, cw_ref, cb_ref, wa_ref, ba_ref, wx_ref, bx_ref, lam_ref, h0_ref, _,
         rec_ref, xp_s, xc_s, a_s, u_s, h_s) = refs
    else:
        (xr_ref, xg_ref, cw_ref, cb_ref, wa_ref, ba_ref, wx_ref, bx_ref, lam_ref, _,
         rec_ref, st_ref, xp_s, xc_s, a_s, u_s, h_s) = refs
    pad = SUBLANES
    left = (ntap - 1) // 2
    nblk = wc // LANES
    nchunks = t // rc
    ntiles = t // SUBLANES

    xp_s[0:pad, :] = jnp.zeros((pad, wc), F32)
    xp_s[pad + t:pad + t + pad, :] = jnp.zeros((pad, wc), F32)
    xp_s[pad:pad + t, :] = xr_ref[...]

    for r0 in range(0, t, rc):
        acc = jnp.broadcast_to(cb_ref[...], (rc, wc))
        for i in range(ntap):
            base = r0 + pad - left + i
            acc = acc + cw_ref[i:i + 1, :] * xp_s[base:base + rc, :]
        xc_s[r0:r0 + rc, :] = acc

    row = lax.broadcasted_iota(jnp.int32, (SUBLANES, wc), 0)

    for d in range(2):
        reverse = d == 1
        sp = jax.nn.softplus(-lam_ref[d:d + 1, :])

        def gate_chunk(c, carry):
            r0 = pl.multiple_of(c * rc, rc)
            xc = xc_s[pl.ds(r0, rc), :]
            xcb = xc.astype(BF16)
            zs_a, zs_x = [], []
            for n in range(nblk):
                xb = xcb[:, n * LANES:(n + 1) * LANES]
                zs_a.append(jnp.dot(xb, wa_ref[d, n].astype(BF16), preferred_element_type=F32))
                zs_x.append(jnp.dot(xb, wx_ref[d, n].astype(BF16), preferred_element_type=F32))
            rg = _sigmoid_tanh(jnp.concatenate(zs_a, axis=1) + ba_ref[d:d + 1, :])
            ig = _sigmoid_tanh(jnp.concatenate(zs_x, axis=1) + bx_ref[d:d + 1, :])
            a = jnp.exp((-LRU_C) * rg * sp)
            a_s[pl.ds(r0, rc), :] = a
            u_s[pl.ds(r0, rc), :] = jnp.sqrt(1.0 - a * a) * (ig * xc)
            return carry

        lax.fori_loop(0, nchunks, gate_chunk, 0)

        def scan_tile(i, h_prev):
            tile = (ntiles - 1 - i) if reverse else i
            r0 = pl.multiple_of(tile * SUBLANES, SUBLANES)
            a = a_s[pl.ds(r0, SUBLANES), :]
            u = u_s[pl.ds(r0, SUBLANES), :]
            for s in (1, 2, 4):
                if reverse:
                    keep = row < SUBLANES - s
                    shift = SUBLANES - s
                else:
                    keep = row >= s
                    shift = s
                a_sh = jnp.where(keep, pltpu.roll(a, shift, 0), 1.0)
                u_sh = jnp.where(keep, pltpu.roll(u, shift, 0), 0.0)
                u = u + a * u_sh
                a = a * a_sh
            h = u + a * h_prev
            if reverse:
                h_s[pl.ds(r0, SUBLANES), :] = h_s[pl.ds(r0, SUBLANES), :] + h
                return h[0:1, :]
            h_s[pl.ds(r0, SUBLANES), :] = h
            return h[SUBLANES - 1:SUBLANES, :]

        h_init = h0_ref[0, d:d + 1, :] if has_h0 else jnp.zeros((1, wc), F32)
        h_last = lax.fori_loop(0, ntiles, scan_tile, h_init)
        if not has_h0:
            st_ref[0, d:d + 1, :] = h_last

    def out_chunk(c, carry):
        r0 = pl.multiple_of(c * rc, rc)
        rec_ref[pl.ds(r0, rc), :] = (h_s[pl.ds(r0, rc), :] * _gelu_tanh(xg_ref[pl.ds(r0, rc), :])).astype(BF16)
        return carry

    lax.fori_loop(0, nchunks, out_chunk, 0)


def _lru(z, col0, row_blk0, nb, t, width, cw, cb, wa, ba, wx, bx, lam, h0, buf=None):
    wc = _pick(width, 512)
    ncb = width // wc
    ntap = cw.shape[0]
    nblk = wc // LANES
    has_h0 = h0 is not None
    rc = _pick(t, 128, SUBLANES)
    in_specs = [pl.BlockSpec((t, wc), lambda b, c: (row_blk0 + b, col0 // wc + c)),
                pl.BlockSpec((t, wc), lambda b, c: (row_blk0 + b, (col0 + width) // wc + c)),
                pl.BlockSpec((ntap, wc), lambda b, c: (0, c)),
                pl.BlockSpec((1, wc), lambda b, c: (0, c)),
                pl.BlockSpec((2, nblk, LANES, LANES), lambda b, c: (0, c, 0, 0)),
                pl.BlockSpec((2, wc), lambda b, c: (0, c)),
                pl.BlockSpec((2, nblk, LANES, LANES), lambda b, c: (0, c, 0, 0)),
                pl.BlockSpec((2, wc), lambda b, c: (0, c)),
                pl.BlockSpec((2, wc), lambda b, c: (0, c))]
    args = [z, z, cw, cb.reshape(1, width), wa, ba, wx, bx, lam]
    out_specs = [pl.BlockSpec((t, wc), lambda b, c: (row_blk0 + b, c))]
    out_shape = [jax.ShapeDtypeStruct((z.shape[0], width), BF16)]
    if has_h0:
        in_specs.append(pl.BlockSpec((1, 2, wc), lambda b, c: (b, 0, c)))
        args.append(h0)
    else:
        buf = _shared_buffer(z.shape[0], width)
        out_specs.append(pl.BlockSpec((1, 2, wc), lambda b, c: (b, 0, c)))
        out_shape.append(jax.ShapeDtypeStruct((nb, 2, width), F32))
    in_specs.append(pl.BlockSpec(memory_space=pl.ANY))
    args.append(buf)
    aliases = {len(args) - 1: 0}
    out = pl.pallas_call(
        functools.partial(_lru_kernel, t, wc, ntap, has_h0, rc),
        grid=(nb, ncb),
        in_specs=in_specs,
        out_specs=out_specs,
        out_shape=out_shape,
        input_output_aliases=aliases,
        scratch_shapes=[pltpu.VMEM((t + 2 * SUBLANES, wc), F32)] + [pltpu.VMEM((t, wc), F32)] * 4,
        compiler_params=_cparams(("arbitrary", "arbitrary"), 12 * t * wc * 4 / MIB + 10),
        name="rglru",
    )(*args)
    return out if not has_h0 else (out[0], None)


_HALO = 16


def _conv_module_kernel(st, tb, ntap, cch, rch, v_ref, g_ref, vp_ref, gp_ref, vn_ref, gn_ref,
                        w_ref, b_ref, lg_ref, lb_ref, o_ref, u_s, y_s, wb_s):
    i = pl.program_id(0)
    pos = i * tb
    in_ctx = pos < st.mp
    off = jnp.where(in_ctx, pos % st.seq, (pos - st.mp) % st.dec_seq)
    length = jnp.where(in_ctx, st.seq, st.dec_seq)
    has_prev = (off != 0).astype(F32)
    has_next = (off + tb != length).astype(F32)
    c = v_ref.shape[1]
    left = (ntap - 1) // 2
    nrc = tb // rch
    span = tb + 2 * _HALO - SUBLANES

    u_s[0, 0:_HALO, :] = vp_ref[...] * _sigmoid_tanh(gp_ref[...]) * has_prev
    u_s[0, _HALO + tb:_HALO + tb + _HALO, :] = vn_ref[...] * _sigmoid_tanh(gn_ref[...]) * has_next

    def glu_chunk(ci, carry):
        r0 = pl.multiple_of(ci * rch, rch)
        u_s[0, pl.ds(_HALO + r0, rch), :] = v_ref[pl.ds(r0, rch), :] * _sigmoid_tanh(g_ref[pl.ds(r0, rch), :])
        return carry

    lax.fori_loop(0, nrc, glu_chunk, 0)

    for s in range(1, SUBLANES):
        u_s[s, 0:span, :] = u_s[0, s:s + span, :]

    @pl.when(i == 0)
    def _():
        for k in range(ntap):
            wb_s[k * SUBLANES:(k + 1) * SUBLANES, :] = jnp.broadcast_to(w_ref[k:k + 1, :], (SUBLANES, c))

    nsub = rch // SUBLANES
    for c0 in range(0, c, cch):
        def conv_chunk(ci, carry):
            r0 = pl.multiple_of(ci * rch, rch)
            acc = jnp.broadcast_to(b_ref[:, c0:c0 + cch], (nsub, SUBLANES, cch))
            for k in range(ntap):
                off = _HALO - left + k
                rows = pl.ds(pl.multiple_of(r0 + off // SUBLANES * SUBLANES, SUBLANES), rch)
                u = u_s[off % SUBLANES, rows, c0:c0 + cch].reshape(nsub, SUBLANES, cch)
                acc = acc + wb_s[k * SUBLANES:(k + 1) * SUBLANES, c0:c0 + cch][None] * u
            y_s[pl.ds(r0, rch), c0:c0 + cch] = acc.reshape(rch, cch)
            return carry

        lax.fori_loop(0, nrc, conv_chunk, 0)

    nch = _pick(tb, 128, SUBLANES)

    def norm_chunk(ci, carry):
        r0 = pl.multiple_of(ci * nch, nch)
        y = y_s[pl.ds(r0, nch), :]
        yc = y - jnp.mean(y, axis=-1, keepdims=True)
        var = jnp.mean(yc * yc, axis=-1, keepdims=True)
        yn = yc * lax.rsqrt(var + EPS) * lg_ref[...] + lb_ref[...]
        o_ref[pl.ds(r0, nch), :] = (yn * _sigmoid_tanh(yn)).astype(BF16)
        return carry

    lax.fori_loop(0, tb // nch, norm_chunk, 0)


def _conv_module(st, z, c, w, b, ln_g, ln_b, tb=256):
    m = z.shape[0]
    ntap = w.shape[0]
    assert (ntap - 1) // 2 < _HALO and ntap // 2 < _HALO
    hpb = tb // _HALO
    nhb = m // _HALO
    prev_map = lambda col: (lambda i: (jnp.maximum(i * hpb - 1, 0), col))
    next_map = lambda col: (lambda i: (jnp.minimum((i + 1) * hpb, nhb - 1), col))
    vec = lambda: pl.BlockSpec((1, c), lambda i: (0, 0))
    return pl.pallas_call(
        functools.partial(_conv_module_kernel, st, tb, ntap, _pick(c, 512), 64),
        grid=(m // tb,),
        in_specs=[pl.BlockSpec((tb, c), lambda i: (i, 0)),
                  pl.BlockSpec((tb, c), lambda i: (i, 1)),
                  pl.BlockSpec((_HALO, c), prev_map(0)),
                  pl.BlockSpec((_HALO, c), prev_map(1)),
                  pl.BlockSpec((_HALO, c), next_map(0)),
                  pl.BlockSpec((_HALO, c), next_map(1)),
                  pl.BlockSpec((ntap, c), lambda i: (0, 0)),
                  vec(), vec(), vec()],
        out_specs=pl.BlockSpec((tb, c), lambda i: (i, 0)),
        out_shape=jax.ShapeDtypeStruct((m, c), BF16),
        scratch_shapes=[pltpu.VMEM((SUBLANES, tb + 2 * _HALO, c), F32), pltpu.VMEM((tb, c), F32),
                        pltpu.VMEM((ntap * SUBLANES, c), F32)],
        compiler_params=_cparams(("arbitrary",), (SUBLANES * (tb + 2 * _HALO) + 6 * tb) * c * 4 / MIB + 8),
        name="conv_module",
    )(z, z, z, z, z, z, w, b.reshape(1, c), ln_g.reshape(1, c), ln_b.reshape(1, c))


def _mla_norm_kernel(cq_ref, ckv_ref, gq_ref, gkv_ref, q_ref, kv_ref, kvb_ref):
    q_ref[...] = _rms(cq_ref[...], gq_ref[...]).astype(BF16)
    kv = _rms(ckv_ref[...], gkv_ref[...])
    kv_ref[...] = kv
    kvb_ref[...] = kv.astype(BF16)


def _mla_norm(z, col_q, ql, col_kv, kvl, gq, gkv, tb=256):
    m = z.shape[0]
    return pl.pallas_call(
        _mla_norm_kernel,
        grid=(m // tb,),
        in_specs=[pl.BlockSpec((tb, ql), lambda i: (i, col_q // ql)),
                  pl.BlockSpec((tb, kvl), lambda i: (i, col_kv // kvl)),
                  pl.BlockSpec((1, ql), lambda i: (0, 0)),
                  pl.BlockSpec((1, kvl), lambda i: (0, 0))],
        out_specs=[pl.BlockSpec((tb, ql), lambda i: (i, 0)),
                   pl.BlockSpec((tb, kvl), lambda i: (i, 0)),
                   pl.BlockSpec((tb, kvl), lambda i: (i, 0))],
        out_shape=[jax.ShapeDtypeStruct((m, ql), BF16),
                   jax.ShapeDtypeStruct((m, kvl), F32),
                   jax.ShapeDtypeStruct((m, kvl), BF16)],
        compiler_params=_cparams(("arbitrary",), 16),
        name="mla_norm",
    )(z, z, gq.reshape(1, ql), gkv.reshape(1, kvl))


def _swap_halves(x, quarter):
    n = x.shape[-1]
    lane = lax.broadcasted_iota(jnp.int32, x.shape, x.ndim - 1)
    first = (lane % (2 * quarter)) < quarter
    return jnp.where(first, pltpu.roll(x, n - quarter, x.ndim - 1), pltpu.roll(x, quarter, x.ndim - 1))


def _rope_kernel(quarter, rope, qpe_ref, kpe_ref, cq_ref, sq_ref, ck_ref, sk_ref, qo_ref, ko_ref):
    q = qpe_ref[...]
    qo_ref[...] = (q * cq_ref[...] + _swap_halves(q, quarter) * sq_ref[...]).astype(BF16)
    k = kpe_ref[...]
    kr = k * ck_ref[...] + _swap_halves(k, quarter) * sk_ref[...]
    ko_ref[...] = (kr + pltpu.roll(kr, rope, 1)).astype(BF16)


def _rope_tables(t, rope, nrep):
    tok = jnp.arange(t)
    pos = jnp.stack([tok // GRID_W, tok % GRID_W], axis=-1).astype(F32)
    nf = rope // 4
    inv = ROPE_BASE ** (-jnp.arange(nf, dtype=F32) / nf)
    ang = pos[:, :, None] * inv
    cos, sin = jnp.cos(ang), jnp.sin(ang)
    cos_t = jnp.concatenate([cos[:, 0], cos[:, 0], cos[:, 1], cos[:, 1]], axis=-1)
    sin_t = jnp.concatenate([-sin[:, 0], sin[:, 0], -sin[:, 1], sin[:, 1]], axis=-1)
    return jnp.tile(cos_t, (1, nrep)), jnp.tile(sin_t, (1, nrep))


def _rope(q, col_pe, pe_w, kpe, row0, ms, t, rope, tb=256):
    assert 2 * rope == LANES and kpe.shape[1] == LANES
    cq, sq = _rope_tables(t, rope, pe_w // rope)
    ck, sk = _rope_tables(t, rope, 1)
    zpad = jnp.zeros((t, LANES - rope), F32)
    ck = jnp.concatenate([ck, zpad], axis=1)
    sk = jnp.concatenate([sk, zpad], axis=1)
    rb0 = row0 // tb
    tpb = t // tb
    return pl.pallas_call(
        functools.partial(_rope_kernel, rope // 4, rope),
        grid=(ms // tb,),
        in_specs=[pl.BlockSpec((tb, pe_w), lambda i: (rb0 + i, col_pe // pe_w)),
                  pl.BlockSpec((tb, LANES), lambda i: (rb0 + i, 0)),
                  pl.BlockSpec((tb, pe_w), lambda i: (i % tpb, 0)),
                  pl.BlockSpec((tb, pe_w), lambda i: (i % tpb, 0)),
                  pl.BlockSpec((tb, LANES), lambda i: (i % tpb, 0)),
                  pl.BlockSpec((tb, LANES), lambda i: (i % tpb, 0))],
        out_specs=[pl.BlockSpec((tb, pe_w), lambda i: (i, 0)),
                   pl.BlockSpec((tb, LANES), lambda i: (i, 0))],
        out_shape=[jax.ShapeDtypeStruct((ms, pe_w), BF16),
                   jax.ShapeDtypeStruct((ms, LANES), BF16)],
        compiler_params=_cparams(("arbitrary",), 24),
        name="rope",
    )(q, kpe, cq, sq, ck, sk)


def _mla_ctx_kernel(nh, dn, dv, rope, qn_ref, qpe_ref, kv_ref, kpe_ref, _, o_ref):
    scale = (dn + rope) ** -0.5
    kpe = kpe_ref[...]
    kpe2 = (kpe + pltpu.roll(kpe, rope, 1)).astype(BF16)
    lane = lax.broadcasted_iota(jnp.int32, (qpe_ref.shape[0], LANES), 1)
    for h in range(nh):
        qn = qn_ref[:, h * dn:(h + 1) * dn].astype(BF16)
        kn = kv_ref[:, h * (dn + dv):h * (dn + dv) + dn]
        v = kv_ref[:, h * (dn + dv) + dn:(h + 1) * (dn + dv)]
        blk = (h * rope) // LANES
        half = (h * rope) % LANES // rope
        qp = qpe_ref[:, blk * LANES:(blk + 1) * LANES]
        qp = jnp.where(lane // rope == half, qp, 0.0).astype(BF16)
        s = _dot_nt(jnp.concatenate([qn, qp], axis=1), jnp.concatenate([kn, kpe2], axis=1)) * scale
        o_ref[:, h * dv:(h + 1) * dv] = _softmax_pv([s], [v]).astype(BF16)


def _mla_ctx_attention(q, kv, kpe, nb, seq, nh, dn, dv, rope):
    assert dn == dv
    return pl.pallas_call(
        functools.partial(_mla_ctx_kernel, nh, dn, dv, rope),
        grid=(nb,),
        in_specs=[pl.BlockSpec((seq, nh * dn), lambda b: (b, 0)),
                  pl.BlockSpec((seq, nh * rope), lambda b: (b, dn // rope)),
                  pl.BlockSpec((seq, nh * (dn + dv)), lambda b: (b, 0)),
                  pl.BlockSpec((seq, LANES), lambda b: (b, 0)),
                  pl.BlockSpec(memory_space=pl.ANY)],
        out_specs=pl.BlockSpec((seq, nh * dv), lambda b: (b, 0)),
        out_shape=jax.ShapeDtypeStruct((q.shape[0], nh * dv), BF16),
        input_output_aliases={4: 0},
        compiler_params=_cparams(("arbitrary",), 32),
        name="mla_ctx_attention",
    )(q, q, kv, kpe, _shared_buffer(q.shape[0], nh * dv))


def _mla_lat_kernel(t, qc, dn, dv, rope, qn_ref, qpe_ref, kv_ref, kvc_ref, kpe_ref, kpec_ref, _, o_ref):
    scale = (dn + rope) ** -0.5
    hps = LANES // rope
    kpe = kpe_ref[...]
    kpec = kpec_ref[0]
    lane = lax.broadcasted_iota(jnp.int32, (qc, LANES), 1)
    for hh in range(hps):
        c0 = hh * (dn + dv)
        v, vc = kv_ref[:, c0 + dn:c0 + dn + dv], kvc_ref[:, c0 + dn:c0 + dn + dv]
        k_own = jnp.concatenate([kv_ref[:, c0:c0 + dn], kpe], axis=1)
        k_ctx = jnp.concatenate([kvc_ref[:, c0:c0 + dn], kpec], axis=1)
        for r0 in range(0, t, qc):
            qn = qn_ref[r0:r0 + qc, hh * dn:(hh + 1) * dn].astype(BF16)
            qp = jnp.where(lane // rope == hh, qpe_ref[r0:r0 + qc, :], jnp.zeros((), BF16))
            q = jnp.concatenate([qn, qp], axis=1)
            s_own = _dot_nt(q, k_own) * scale
            s_ctx = _dot_nt(q, k_ctx) * scale
            o_ref[r0:r0 + qc, hh * dv:(hh + 1) * dv] = _softmax_pv([s_own, s_ctx], [v, vc]).astype(BF16)


def _mla_lat_attention(q, row_blk0, qpe, kv, kv_ctx, kpe, kpe_ctx, nb, t, nh, dn, dv, rope, buf):
    assert dn == dv == LANES
    hps = LANES // rope
    past = kv_ctx.shape[0] // nb
    return pl.pallas_call(
        functools.partial(_mla_lat_kernel, t, _pick(t, 256, SUBLANES), dn, dv, rope),
        grid=(nb, nh // hps),
        in_specs=[pl.BlockSpec((t, hps * dn), lambda b, j: (row_blk0 + b, j)),
                  pl.BlockSpec((t, LANES), lambda b, j: (b, j)),
                  pl.BlockSpec((t, hps * (dn + dv)), lambda b, j: (row_blk0 + b, j)),
                  pl.BlockSpec((past, hps * (dn + dv)), lambda b, j: (b, j)),
                  pl.BlockSpec((t, LANES), lambda b, j: (b, 0)),
                  pl.BlockSpec((1, past, LANES), lambda b, j: (b, 0, 0)),
                  pl.BlockSpec(memory_space=pl.ANY)],
        out_specs=pl.BlockSpec((t, hps * dv), lambda b, j: (row_blk0 + b, j)),
        out_shape=jax.ShapeDtypeStruct(buf.shape, buf.dtype),
        input_output_aliases={6: 0},
        compiler_params=_cparams(("arbitrary", "arbitrary"), 32),
        name="mla_latent_attention",
    )(q, qpe, kv, kv_ctx, kpe, kpe_ctx, buf)


def _mixer_ab(st, h, nb_ctx, nb_lat, k_ctx, v_ctx, s0, w_in, rpb, conv_w, conv_b, wa, ba, wx, bx, lam, w_out):
    nh, hd = k_ctx.shape[1], k_ctx.shape[3]
    na = nh * hd
    width = lam.shape[1]
    z = _ws_matmul(h, w_in[None], 0, w_in.shape[1], name="w_in0")[0]
    attn, new_k, new_v = _na_ctx_attention(z, nb_ctx, st.seq, nh, hd)
    attn = _na_lat_attention(z, st.mp // st.dec_seq, nb_lat, st.dec_seq, nh, hd, k_ctx, v_ctx, rpb, attn)
    rec, new_s = _lru(z, 3 * na, 0, nb_ctx, st.seq, width, conv_w, conv_b, wa, ba, wx, bx, lam, None)
    rec, _ = _lru(z, 3 * na, st.mp // st.dec_seq, nb_lat, st.dec_seq, width,
                  conv_w, conv_b, wa, ba, wx, bx, lam, s0, rec)
    y = _ws_matmul([attn, rec], w_out[None], 0, w_out.shape[1], name="w_out0")
    return y, (new_k, new_v, new_s)


def _mixer_cd(st, h, nb_ctx, nb_lat, ckv_ctx, kpe_ctx, w_in, dw_w, dw_b, ln_g, ln_b, qn_g, w_uq, kvn_g,
              w_ukv, w_out):
    cw = dw_w.shape[1]
    ql = qn_g.shape[0]
    kvl = kvn_g.shape[0]
    rope = kpe_ctx.shape[2]
    dn = dv = LANES
    nh = w_ukv.shape[1] // (dn + dv)
    n_main = 2 * cw + ql + kvl
    z = _ws_matmul(h, w_in[None], 0, n_main, name="w_in1")[0]
    kpe = _ws_matmul(h, w_in[None], 0, LANES, n_off=n_main, bn=LANES, name="w_kpe")[0]

    u = _conv_module(st, z, cw, dw_w, dw_b, ln_g, ln_b)

    cq_n, ckv_n, ckv_nb = _mla_norm(z, 2 * cw, ql, 2 * cw + ql, kvl, qn_g, kvn_g)
    w_uq3 = w_uq.reshape(ql, nh, dn + rope)
    w_uq_perm = jnp.concatenate([w_uq3[:, :, :dn].reshape(ql, nh * dn),
                                 w_uq3[:, :, dn:].reshape(ql, nh * rope)], axis=1)
    q = _ws_matmul(cq_n, w_uq_perm[None], 0, w_uq_perm.shape[1], bn=1024, name="w_uq")[0]
    kv = _ws_matmul(ckv_nb, w_ukv[None], 0, w_ukv.shape[1], bn=1024, out_dtype=BF16, name="w_ukv")[0]
    past = ckv_ctx.shape[1]
    kv_ctx = _ws_matmul(ckv_ctx.reshape(nb_lat * past, kvl).astype(BF16), w_ukv[None], 0, w_ukv.shape[1],
                        bn=1024, out_dtype=BF16, name="w_ukv_ctx")[0]
    qpe_l, kpe_l = _rope(q, nh * dn, nh * rope, kpe, st.mp, st.ms, st.dec_seq, rope)
    kpe_c2 = jnp.concatenate([kpe_ctx, kpe_ctx], axis=-1).astype(BF16)
    o = _mla_ctx_attention(q, kv, kpe, nb_ctx, st.seq, nh, dn, dv, rope)
    o = _mla_lat_attention(q, st.mp // st.dec_seq, qpe_l, kv, kv_ctx, kpe_l, kpe_c2,
                           nb_lat, st.dec_seq, nh, dn, dv, rope, o)
    y = _ws_matmul([u, o], w_out[None], 0, w_out.shape[1], name="w_out1")
    new_ckv = ckv_n[:st.mp].reshape(nb_ctx, st.seq, kvl)
    new_kpe = kpe[:st.mp, :rope].reshape(nb_ctx, st.seq, rope)
    return y, (new_ckv, new_kpe)


def _ffn(h, w1, w2, f):
    a = _ws_matmul(h, w1, f, w1.shape[2] // 2, bm=2048, bn=256, swiglu=True, out_dtype=BF16,
                   name="matmul_swiglu")[0]
    y = _ws_matmul(a, w2, f, w2.shape[2], bm=1024, bn=512, ksplit=(0, 2), name="ffn_w2a")
    return _ws_matmul(a, w2, f, w2.shape[2], bm=1024, bn=512, ksplit=(1, 2), acc=y, name="ffn_w2b")


def kernel(x_prompt, x_sample, c, cache_l0_na_k, cache_l0_na_v, state_l0_lru, cache_l1_mla_ckv, cache_l1_mla_kpe, c_ctx, ada_w0, ada_b0, norm_g0, ffn_w1_0, ffn_w2_0, w_in0, na_rpb0, lru_conv_w0, lru_conv_b0, lru_wa0, lru_ba0, lru_wx0, lru_bx0, lru_lam0, w_out0, ada_w1, ada_b1, norm_g1, ffn_w1_1, ffn_w2_1, w_in1, cv_dw_w1, cv_dw_b1, cv_ln_g1, cv_ln_b1, mla_qnorm_g1, mla_w_uq1, mla_kvnorm_g1, mla_w_ukv1, w_out1):
    nb_ctx, seq, d = x_prompt.shape
    nb_lat, dec_seq, _ = x_sample.shape
    st = _Stream(nb_ctx * seq, seq, nb_lat * dec_seq, dec_seq)
    assert st.mp % dec_seq == 0 and dec_seq % seq == 0

    nrow = -(-(1 + nb_lat) // SUBLANES) * SUBLANES
    cvec = jnp.concatenate([c_ctx[None], c, jnp.zeros((nrow - 1 - nb_lat, d), F32)], axis=0)
    mods = [_modulation(cvec, aw, ab).reshape(nrow * 3 * N_SUB, 1, d)
            for aw, ab in ((ada_w0, ada_b0), (ada_w1, ada_b1))]
    norms = [g.reshape(2 * N_SUB, 1, d) for g in (norm_g0, norm_g1)]
    ffns = [(ffn_w1_0, ffn_w2_0), (ffn_w1_1, ffn_w2_1)]

    x = (x_prompt.reshape(st.mp, d), x_sample.reshape(st.ms, d))
    h = _pre(st, x, mods[0], norms[0], 0)
    states = []
    for l in range(2):
        mod3, norm3 = mods[l], norms[l]
        w1, w2 = ffns[l]
        y = _ffn(h, w1, w2, 0)
        x, h = _postpre(st, x, y, mod3, norm3, 0, 0.5, nxt=(mod3, norm3, 1))
        if l == 0:
            y, state = _mixer_ab(st, h, nb_ctx, nb_lat, cache_l0_na_k, cache_l0_na_v, state_l0_lru,
                                 w_in0, na_rpb0, lru_conv_w0, lru_conv_b0, lru_wa0, lru_ba0, lru_wx0, lru_bx0,
                                 lru_lam0, w_out0)
        else:
            y, state = _mixer_cd(st, h, nb_ctx, nb_lat, cache_l1_mla_ckv, cache_l1_mla_kpe,
                                 w_in1, cv_dw_w1, cv_dw_b1, cv_ln_g1, cv_ln_b1, mla_qnorm_g1, mla_w_uq1,
                                 mla_kvnorm_g1, mla_w_ukv1, w_out1)
        states.append(state)
        x, h = _postpre(st, x, y, mod3, norm3, 1, 1.0, nxt=(mod3, norm3, 2))
        y = _ffn(h, w1, w2, 1)
        if l == 0:
            x, h = _postpre(st, x, y, mod3, norm3, 2, 0.5, nxt=(mods[1], norms[1], 0))

    y_p, _ = _postpre(st, x, y, mod3, norm3, 2, 0.5, rows=(0, st.mp))
    y_s, _ = _postpre(st, x, y, mod3, norm3, 2, 0.5, rows=(st.mp, st.ms))
    (new_k, new_v, new_s), (new_ckv, new_kpe) = states
    return (y_p.reshape(nb_ctx, seq, d), y_s.reshape(nb_lat, dec_seq, d), new_k, new_v, new_s, new_ckv, new_kpe)
```
